```python
import math
import jax, jax.numpy as jnp
from jax import lax
import numpy as np

D_MODEL = 1024
BATCH = 16
SEQ = 4096
DEPTH = 2
DEC_BATCH = 32
DEC_SEQ = 16
PAST_LEN = 1024

CHUNK = 64
D_MIX = 2 * D_MODEL
HEAD_DIM = 64
D_A = 3 * D_MIX // 8
H_A = D_A // HEAD_DIM
R_W = 64
R_A = 64
GN_EPS = 64e-5
D_B = 3 * D_MIX // 8
H_B = D_B // HEAD_DIM
N_GROUPS = 2
HPG = H_B // N_GROUPS
D_STATE = 128
CONV_W = 4
CONV_DIM = D_B + 2 * N_GROUPS * D_STATE
D_C = D_MIX - D_A - D_B
H_C = D_C // HEAD_DIM
SB_BLOCK = 128
W_SHIFT = 3 * D_A + R_W + R_A
O1 = W_SHIFT
O2 = O1 + D_A
O3 = O2 + D_B
O4 = O3 + CONV_DIM
O5 = O4 + H_B
O6 = O5 + D_C
O7 = O6 + D_C
O8 = O7 + D_C
N_IN = O8 + D_C
IN_SPLITS = (O1, O2, O3, O4, O5, O6, O7, O8)

kernel_name = 'hybrid_rwkv7_mamba2_stickbreaking_stream_step'


def rms_norm(x, w, eps=1e-6):
    xf = x.astype(jnp.float32)
    y = xf * lax.rsqrt(jnp.mean(xf * xf, axis=-1, keepdims=True) + eps)
    return (y * w.astype(jnp.float32)).astype(x.dtype)


def token_shift(u, prev, mu):
    u_prev = jnp.concatenate([prev.astype(u.dtype), u[:, :-1]], axis=1)
    return u + (u_prev - u) * mu, u[:, -1:]


def rwkv7_branch(ua, gate, S0, shift0, p):
    B, T, _ = ua.shape
    f32 = jnp.float32
    us, new_shift = token_shift(ua, shift0, p['rwkv_mu'])
    r, k, v, w_lo, a_lo = jnp.split(us, [D_A, 2 * D_A, 3 * D_A, 3 * D_A + R_W], axis=-1)
    w_log = -jax.nn.softplus(-(p['rwkv_w0'] + jnp.tanh(w_lo) @ p['rwkv_w2']).astype(f32)) - 0.5
    decay = jnp.exp(-jnp.exp(w_log))
    a = jax.nn.sigmoid((p['rwkv_a0'] + a_lo @ p['rwkv_a2']).astype(f32))
    heads = lambda t: t.astype(f32).reshape(B, T, H_A, HEAD_DIM)
    r, k, v, decay, a = heads(r), heads(k), heads(v), heads(decay), heads(a)
    kk = k * p['rwkv_k_k'].astype(f32).reshape(H_A, HEAD_DIM)
    kk = kk / jnp.maximum(jnp.sqrt(jnp.sum(kk * kk, axis=-1, keepdims=True)), 1e-12)
    k = k * (1.0 + (a - 1.0) * p['rwkv_k_a'].astype(f32).reshape(H_A, HEAD_DIM))

    def step(S, inp):
        r_t, w_t, k_t, v_t, rm_t, ad_t = inp
        sa = jnp.einsum('bhij,bhj->bhi', S, rm_t)
        S = S * w_t[:, :, None, :] + sa[..., None] * ad_t[:, :, None, :] + v_t[..., None] * k_t[:, :, None, :]
        return S, jnp.einsum('bhij,bhj->bhi', S, r_t)

    seq = tuple(t.swapaxes(0, 1) for t in (r, decay, k, v, -kk, kk * a))
    S_T, y = lax.scan(step, S0.astype(f32), seq)
    y = y.swapaxes(0, 1)
    mean = jnp.mean(y, axis=-1, keepdims=True)
    var = jnp.mean(jnp.square(y - mean), axis=-1, keepdims=True)
    y = (y - mean) * lax.rsqrt(var + GN_EPS) * p['rwkv_ln_w'].astype(f32).reshape(H_A, HEAD_DIM) \
        + p['rwkv_ln_b'].astype(f32).reshape(H_A, HEAD_DIM)
    y = y + jnp.sum(r * k * p['rwkv_r_k'].astype(f32), axis=-1, keepdims=True) * v
    out = y.reshape(B, T, D_A) * jax.nn.silu(gate.astype(f32))
    return out.astype(ua.dtype), S_T.astype(S0.dtype), new_shift.astype(shift0.dtype)


def ssd_chunked(x, dt, A, Bm, Cm, S0):
    Bsz, T = x.shape[:2]
    Q = math.gcd(T, CHUNK)
    nC = T // Q
    ch = lambda t: t.reshape(Bsz, nC, Q, *t.shape[2:])
    x, dt, Bm, Cm = ch(x), ch(dt), ch(Bm), ch(Cm)
    xdt = x * dt[..., None]
    a_cs = jnp.cumsum(dt * A, axis=2).transpose(0, 1, 3, 4, 2)
    tril = jnp.tril(jnp.ones((Q, Q), dtype=bool))
    seg = jnp.exp(jnp.where(tril, a_cs[..., :, None] - a_cs[..., None, :], -jnp.inf))
    scores = jnp.einsum('bcqgn,bcsgn->bcgqs', Cm, Bm)[:, :, :, None] * seg
    y_diag = jnp.einsum('bcgmqs,bcsgmp->bcqgmp', scores, xdt)
    decay_to_end = jnp.exp(a_cs[..., -1:] - a_cs)
    chunk_states = jnp.einsum('bcsgn,bcgms,bcsgmp->bcgmpn', Bm, decay_to_end, xdt)
    chunk_decay = jnp.exp(a_cs[..., -1])

    def step(S, inp):
        st, dec = inp
        return S * dec[..., None, None] + st, S

    S_T, S_in = lax.scan(step, S0, (chunk_states.swapaxes(0, 1), chunk_decay.swapaxes(0, 1)))
    S_in = S_in.swapaxes(0, 1)
    y_off = jnp.einsum('bcqgn,bcgmpn,bcgmq->bcqgmp', Cm, S_in, jnp.exp(a_cs))
    return (y_diag + y_off).reshape(Bsz, T, *x.shape[3:]), S_T


def mamba2_branch(z, xbc, dt_raw, S0, conv0, p):
    B, T, _ = xbc.shape
    f32 = jnp.float32
    xpad = jnp.concatenate([conv0.astype(xbc.dtype), xbc], axis=1)
    new_conv = xpad[:, T:]
    cw = p['ssm_conv_w'].astype(f32)
    conv = sum((xpad[:, i:i + T].astype(f32) * cw[i] for i in range(CONV_W)), start=p['ssm_conv_b'].astype(f32))
    xbc = jax.nn.silu(conv)
    xs, Bm, Cm = jnp.split(xbc, [D_B, D_B + N_GROUPS * D_STATE], axis=-1)
    xs = xs.reshape(B, T, N_GROUPS, HPG, HEAD_DIM)
    Bm = Bm.reshape(B, T, N_GROUPS, D_STATE)
    Cm = Cm.reshape(B, T, N_GROUPS, D_STATE)
    dt = jax.nn.softplus(dt_raw.astype(f32) + p['ssm_dt_bias'].astype(f32)).reshape(B, T, N_GROUPS, HPG)
    A = -jnp.exp(p['ssm_A_log'].astype(f32)).reshape(N_GROUPS, HPG)
    S0g = S0.astype(f32).reshape(B, N_GROUPS, HPG, HEAD_DIM, D_STATE)
    y, S_T = ssd_chunked(xs, dt, A, Bm, Cm, S0g)
    y = y + p['ssm_D'].astype(f32).reshape(N_GROUPS, HPG, 1) * xs
    y = y.reshape(B, T, D_B) * jax.nn.silu(z.astype(f32))
    y = rms_norm(y.reshape(B, T, N_GROUPS, D_B // N_GROUPS),
                 p['ssm_norm_w'].reshape(N_GROUPS, D_B // N_GROUPS), 1e-5).reshape(B, T, D_B)
    return y.astype(z.dtype), S_T.reshape(B, H_B, HEAD_DIM, D_STATE).astype(S0.dtype), new_conv.astype(conv0.dtype)


def stick_breaking_block(q, k, v, q_start):
    f32 = jnp.float32
    z = jnp.einsum('bhqd,bhkd->bhqk', q.astype(f32), k.astype(f32)) * (HEAD_DIM ** -0.5)
    q_pos = q_start + jnp.arange(q.shape[2])
    k_pos = jnp.arange(k.shape[2])
    causal = k_pos[None, :] < q_pos[:, None]
    log_keep = jnp.where(causal, jax.nn.log_sigmoid(-z), 0.0)
    log_tail = lax.cumsum(log_keep, axis=3, reverse=True) - log_keep
    att = jnp.where(causal, jnp.exp(jax.nn.log_sigmoid(z) + log_tail), 0.0)
    return jnp.einsum('bhqk,bhkd->bhqd', att, v.astype(f32))


def stick_breaking_branch(q, k, v, gate, k_cache, v_cache, p):
    B, T, _ = q.shape
    heads = lambda t: t.reshape(B, T, H_C, HEAD_DIM).transpose(0, 2, 1, 3)
    q = rms_norm(heads(q), p['sb_q_norm_w'])
    k = rms_norm(heads(k), p['sb_k_norm_w'])
    v = heads(v)
    P = k_cache.shape[2]
    k_all = jnp.concatenate([k_cache.astype(k.dtype), k], axis=2)
    v_all = jnp.concatenate([v_cache.astype(v.dtype), v], axis=2)
    outs = []
    for start in range(0, T, SB_BLOCK):
        end = min(T, start + SB_BLOCK)
        outs.append(stick_breaking_block(q[:, :, start:end], k_all[:, :, :P + end], v_all[:, :, :P + end], P + start))
    o = jnp.concatenate(outs, axis=2).transpose(0, 2, 1, 3).reshape(B, T, D_C)
    o = o * jax.nn.silu(gate.astype(jnp.float32))
    return o.astype(gate.dtype), k, v


def hybrid_layer(x, p, S_rwkv, shift, S_ssm, conv_buf, k_cache, v_cache):
    h = rms_norm(x, p['norm_w'])
    u = h @ p['w_in']
    ua, ga, zb, xbc, dtb, qc, kc, vc, gc = jnp.split(u, IN_SPLITS, axis=-1)
    oa, S_rwkv, shift = rwkv7_branch(ua, ga, S_rwkv, shift, p)
    ob, S_ssm, conv_buf = mamba2_branch(zb, xbc, dtb, S_ssm, conv_buf, p)
    oc, k_new, v_new = stick_breaking_branch(qc, kc, vc, gc, k_cache, v_cache, p)
    y = x + (jnp.concatenate([oa, ob, oc], axis=-1) @ p['w_out']).astype(x.dtype)
    return y, (S_rwkv, shift, S_ssm, conv_buf, k_new, v_new)


def setup_inputs(seed: int = 0) -> dict:
    key = jax.random.key(seed)
    ks = iter(jax.random.split(key, 40))
    f32 = jnp.float32
    nrm = lambda shape, s: s * jax.random.normal(next(ks), shape, f32)
    uni = lambda shape, lo, hi: jax.random.uniform(next(ks), shape, f32, minval=lo, maxval=hi)
    x_prompt = nrm((BATCH, SEQ, D_MODEL), 1.0)
    x_sample = nrm((DEC_BATCH, DEC_SEQ, D_MODEL), 1.0)
    state_rwkv = nrm((DEPTH, DEC_BATCH, H_A, HEAD_DIM, HEAD_DIM), 0.5)
    state_rwkv_shift = nrm((DEPTH, DEC_BATCH, 1, W_SHIFT), 1.0)
    state_ssm = nrm((DEPTH, DEC_BATCH, H_B, HEAD_DIM, D_STATE), 0.1)
    state_conv = nrm((DEPTH, DEC_BATCH, CONV_W - 1, CONV_DIM), 1.0)
    cache_sb_k = nrm((DEPTH, DEC_BATCH, H_C, PAST_LEN, HEAD_DIM), 1.0)
    cache_sb_v = nrm((DEPTH, DEC_BATCH, H_C, PAST_LEN, HEAD_DIM), 1.0)
    norm_w = 1.0 + nrm((DEPTH, D_MODEL), 0.02)
    w_in = nrm((DEPTH, D_MODEL, N_IN), D_MODEL ** -0.5)
    w_out = nrm((DEPTH, D_MIX, D_MODEL), 0.5 * D_MIX ** -0.5)
    rwkv_mu = uni((DEPTH, W_SHIFT), 0.0, 1.0)
    rwkv_w0 = uni((DEPTH, D_A), -4.0, 1.0)
    rwkv_w2 = nrm((DEPTH, R_W, D_A), 0.5 * R_W ** -0.5)
    rwkv_a0 = nrm((DEPTH, D_A), 0.5)
    rwkv_a2 = nrm((DEPTH, R_A, D_A), 0.5 * R_A ** -0.5)
    rwkv_k_k = 0.85 + nrm((DEPTH, D_A), 0.02)
    rwkv_k_a = 1.0 + nrm((DEPTH, D_A), 0.02)
    rwkv_r_k = nrm((DEPTH, H_A, HEAD_DIM), 0.1)
    rwkv_ln_w = 1.0 + nrm((DEPTH, D_A), 0.02)
    rwkv_ln_b = nrm((DEPTH, D_A), 0.02)
    ssm_conv_w = nrm((DEPTH, CONV_W, CONV_DIM), CONV_W ** -0.5)
    ssm_conv_b = nrm((DEPTH, CONV_DIM), 0.02)
    dt0 = jnp.exp(uni((DEPTH, H_B), math.log(1e-3), math.log(1e-1)))
    ssm_dt_bias = dt0 + jnp.log(-jnp.expm1(-dt0))
    ssm_A_log = jnp.log(uni((DEPTH, H_B), 1.0, 16.0))
    ssm_D = 1.0 + nrm((DEPTH, H_B), 0.1)
    ssm_norm_w = 1.0 + nrm((DEPTH, D_B), 0.02)
    sb_q_norm_w = 1.0 + nrm((DEPTH, HEAD_DIM), 0.02)
    sb_k_norm_w = 1.0 + nrm((DEPTH, HEAD_DIM), 0.02)
    return {'x_prompt': x_prompt, 'x_sample': x_sample,
            'state_rwkv': state_rwkv, 'state_rwkv_shift': state_rwkv_shift,
            'state_ssm': state_ssm, 'state_conv': state_conv,
            'cache_sb_k': cache_sb_k, 'cache_sb_v': cache_sb_v,
            'norm_w': norm_w, 'w_in': w_in, 'w_out': w_out,
            'rwkv_mu': rwkv_mu, 'rwkv_w0': rwkv_w0, 'rwkv_w2': rwkv_w2, 'rwkv_a0': rwkv_a0, 'rwkv_a2': rwkv_a2,
            'rwkv_k_k': rwkv_k_k, 'rwkv_k_a': rwkv_k_a, 'rwkv_r_k': rwkv_r_k,
            'rwkv_ln_w': rwkv_ln_w, 'rwkv_ln_b': rwkv_ln_b,
            'ssm_conv_w': ssm_conv_w, 'ssm_conv_b': ssm_conv_b, 'ssm_dt_bias': ssm_dt_bias,
            'ssm_A_log': ssm_A_log, 'ssm_D': ssm_D, 'ssm_norm_w': ssm_norm_w,
            'sb_q_norm_w': sb_q_norm_w, 'sb_k_norm_w': sb_k_norm_w}


def reference(x_prompt, x_sample, state_rwkv, state_rwkv_shift, state_ssm, state_conv, cache_sb_k, cache_sb_v,
              norm_w, w_in, w_out, rwkv_mu, rwkv_w0, rwkv_w2, rwkv_a0, rwkv_a2, rwkv_k_k, rwkv_k_a, rwkv_r_k,
              rwkv_ln_w, rwkv_ln_b, ssm_conv_w, ssm_conv_b, ssm_dt_bias, ssm_A_log, ssm_D, ssm_norm_w,
              sb_q_norm_w, sb_k_norm_w):
    bp, dtp = x_prompt.shape[0], x_prompt.dtype
    zero_rwkv = jnp.zeros((bp, H_A, HEAD_DIM, HEAD_DIM), dtp)
    zero_shift = jnp.zeros((bp, 1, W_SHIFT), dtp)
    zero_ssm = jnp.zeros((bp, H_B, HEAD_DIM, D_STATE), dtp)
    zero_conv = jnp.zeros((bp, CONV_W - 1, CONV_DIM), dtp)
    empty_kv = jnp.zeros((bp, H_C, 0, HEAD_DIM), dtp)

    yp, ys = x_prompt, x_sample
    new_p = [[] for _ in range(6)]
    new_s = [[] for _ in range(6)]
    for l in range(DEPTH):
        p = {'norm_w': norm_w[l], 'w_in': w_in[l], 'w_out': w_out[l],
             'rwkv_mu': rwkv_mu[l], 'rwkv_w0': rwkv_w0[l], 'rwkv_w2': rwkv_w2[l], 'rwkv_a0': rwkv_a0[l],
             'rwkv_a2': rwkv_a2[l], 'rwkv_k_k': rwkv_k_k[l], 'rwkv_k_a': rwkv_k_a[l], 'rwkv_r_k': rwkv_r_k[l],
             'rwkv_ln_w': rwkv_ln_w[l], 'rwkv_ln_b': rwkv_ln_b[l],
             'ssm_conv_w': ssm_conv_w[l], 'ssm_conv_b': ssm_conv_b[l], 'ssm_dt_bias': ssm_dt_bias[l],
             'ssm_A_log': ssm_A_log[l], 'ssm_D': ssm_D[l], 'ssm_norm_w': ssm_norm_w[l],
             'sb_q_norm_w': sb_q_norm_w[l], 'sb_k_norm_w': sb_k_norm_w[l]}
        yp, st_p = hybrid_layer(yp, p, zero_rwkv, zero_shift, zero_ssm, zero_conv, empty_kv, empty_kv)
        ys, st_s = hybrid_layer(ys, p, state_rwkv[l], state_rwkv_shift[l], state_ssm[l], state_conv[l],
                                cache_sb_k[l], cache_sb_v[l])
        for i in range(6):
            new_p[i].append(st_p[i])
            new_s[i].append(st_s[i])
    p_rwkv, p_shift, p_ssm, p_conv, p_k, p_v = [jnp.stack(t) for t in new_p]
    s_rwkv, s_shift, s_ssm, s_conv, s_k, s_v = [jnp.stack(t) for t in new_s]
    return (yp, ys, p_rwkv, p_shift, p_ssm, p_conv, p_k, p_v, s_rwkv, s_shift, s_ssm, s_conv, s_k, s_v)
```

```python
import functools

import jax
import jax.numpy as jnp
from jax import lax
from jax.experimental import pallas as pl
from jax.experimental.pallas import tpu as pltpu

F32 = jnp.float32
BF16 = jnp.bfloat16

D_MODEL = 1024
D_MIX = 2 * D_MODEL
HEAD_DIM = 64
D_A = 768
H_A = D_A // HEAD_DIM
R_W = 64
R_A = 64
GN_EPS = 64e-5
D_B = 768
H_B = D_B // HEAD_DIM
N_GROUPS = 2
D_STATE = 128
CONV_W = 4
CONV_DIM = D_B + 2 * N_GROUPS * D_STATE
D_C = 512
H_C = D_C // HEAD_DIM
W_SHIFT = 3 * D_A + R_W + R_A
SB_SCALE = HEAD_DIM ** -0.5

LANES = 128
PAIR = LANES // HEAD_DIM
CHUNK = 64
VMEM_LIMIT = 56 * 1024 * 1024

N_PAIR_A = H_A // PAIR
N_PAIR_B = H_B // PAIR
N_PAIR_C = H_C // PAIR
DT_PAD = LANES

IN_WIDTHS = (W_SHIFT, D_A, D_B, CONV_DIM, DT_PAD, D_C, D_C, D_C, D_C)


def _cparams(sem):
    return pltpu.CompilerParams(dimension_semantics=sem, vmem_limit_bytes=VMEM_LIMIT)


def _bdot(a, b):
    return jnp.dot(a.astype(BF16), b.astype(BF16), preferred_element_type=F32)


def _bdot_nt(a, b):
    return lax.dot_general(a.astype(BF16), b.astype(BF16), (((1,), (1,)), ((), ())),
                           preferred_element_type=F32)


def _bdot_tn(a, b):
    return lax.dot_general(a.astype(BF16), b.astype(BF16), (((0,), (0,)), ((), ())),
                           preferred_element_type=F32)


def _split_bf16(x, n):
    parts, r = [], x
    for i in range(n):
        p = r.astype(BF16)
        parts.append(p)
        if i + 1 < n:
            r = r - p.astype(F32)
    return parts


def _sel_dot(sel, x, n=3):
    acc = None
    for p in _split_bf16(x, n):
        d = jnp.dot(sel, p, preferred_element_type=F32)
        acc = d if acc is None else acc + d
    return acc


def _dot_sel(x, sel, n=3):
    acc = None
    for p in _split_bf16(x, n):
        d = jnp.dot(p, sel, preferred_element_type=F32)
        acc = d if acc is None else acc + d
    return acc


def _lane_lo(shape):
    return lax.broadcasted_iota(jnp.int32, shape, len(shape) - 1) < HEAD_DIM


def _head_sum(x, lo):
    s_lo = jnp.sum(jnp.where(lo, x, 0.0), axis=-1, keepdims=True)
    s_hi = jnp.sum(jnp.where(lo, 0.0, x), axis=-1, keepdims=True)
    return jnp.where(lo, s_lo, s_hi)


def _stack_heads(x, lo):
    zero = jnp.zeros_like(x)
    return jnp.concatenate([jnp.where(lo, x, zero), jnp.where(lo, zero, x)], axis=0)


def _silu(x):
    return x * jax.nn.sigmoid(x)


def _softplus(x):
    return jnp.maximum(x, 0.0) + jnp.log1p(jnp.exp(-jnp.abs(x)))


def _in_proj_kernel(x_ref, nw_ref, w_ref, *out_refs):
    x = x_ref[...]
    ms = jnp.mean(x * x, axis=-1, keepdims=True)
    h = (x * lax.rsqrt(ms + 1e-6) * nw_ref[...]).astype(BF16)
    off = 0
    for o_ref, wd in zip(out_refs, IN_WIDTHS):
        o_ref[...] = jnp.dot(h, w_ref[:, off:off + wd], preferred_element_type=F32)
        off += wd


def _in_proj(x, norm_w, w_cat, tm):
    n = x.shape[0]
    n_cols = sum(IN_WIDTHS)
    return pl.pallas_call(
        _in_proj_kernel,
        grid=(n // tm,),
        in_specs=[pl.BlockSpec((tm, D_MODEL), lambda i: (i, 0)),
                  pl.BlockSpec((1, D_MODEL), lambda i: (0, 0)),
                  pl.BlockSpec((D_MODEL, n_cols), lambda i: (0, 0))],
        out_specs=[pl.BlockSpec((tm, wd), lambda i: (i, 0)) for wd in IN_WIDTHS],
        out_shape=[jax.ShapeDtypeStruct((n, wd), F32) for wd in IN_WIDTHS],
        compiler_params=_cparams(("parallel",)),
        name="in_proj",
    )(x, norm_w, w_cat)


def _rwkv_kernel(ua_ref, ga_ref, s0_ref, sh0_ref, mu_ref, w0_ref, w2_ref, a0_ref, a2_ref,
                 kkw_ref, kaw_ref, rkw_ref, lnw_ref, lnb_ref, tri_ref,
                 oa_ref, st_ref, sht_ref,
                 s_scr, prev_scr, us_scr, *, n_t, last_valid):
    C = CHUNK
    t = pl.program_id(1)

    @pl.when(t == 0)
    def _():
        s_scr[...] = s0_ref[0]
        prev_scr[...] = sh0_ref[0]

    u = ua_ref[...]
    row = lax.broadcasted_iota(jnp.int32, (C, 1), 0)
    u_prev = jnp.where(row == 0, prev_scr[...], pltpu.roll(u, 1, axis=0))
    us_scr[...] = u + (u_prev - u) * mu_ref[...]
    prev_scr[...] = u[C - 1:C, :]

    padded = last_valid < C
    valid = jnp.logical_or(t < n_t - 1, row < last_valid) if padded else None

    w_lo = us_scr[:, 3 * D_A:3 * D_A + R_W]
    a_lo = us_scr[:, 3 * D_A + R_W:W_SHIFT]
    wl = w0_ref[...] + _bdot(jnp.tanh(w_lo), w2_ref[...])
    lw = -jnp.exp(-_softplus(-wl) - 0.5)
    a_all = jax.nn.sigmoid(a0_ref[...] + _bdot(a_lo, a2_ref[...]))
    if padded:
        lw = jnp.where(valid, lw, 0.0)
    cum_all = _sel_dot(tri_ref[...], lw)

    lo = _lane_lo((C, LANES))
    ri = lax.broadcasted_iota(jnp.int32, (2 * C, 2 * C), 0) & (C - 1)
    ci = lax.broadcasted_iota(jnp.int32, (2 * C, 2 * C), 1) & (C - 1)
    strict = ci < ri
    incl = ci <= ri

    for p in range(N_PAIR_A):
        sl = slice(p * LANES, (p + 1) * LANES)
        r = us_scr[:, p * LANES:(p + 1) * LANES]
        k = us_scr[:, D_A + p * LANES:D_A + (p + 1) * LANES]
        v = us_scr[:, 2 * D_A + p * LANES:2 * D_A + (p + 1) * LANES]
        a = a_all[:, sl]
        cum = cum_all[:, sl]
        kk = k * kkw_ref[:, sl]
        kkn = kk / jnp.maximum(jnp.sqrt(_head_sum(kk * kk, lo)), 1e-12)
        k2 = k * (1.0 + (a - 1.0) * kaw_ref[:, sl])
        vv = v
        if padded:
            kkn = jnp.where(valid, kkn, 0.0)
            k2 = jnp.where(valid, k2, 0.0)
            vv = jnp.where(valid, v, 0.0)
        cum_last = cum[C - 1:C, :]
        w_inc = jnp.exp(cum)
        w_exc = jnp.exp(cum - lw[:, sl])
        w_inv = jnp.exp(-cum)
        w_end = jnp.exp(cum_last - cum)
        w_all = jnp.exp(cum_last)
        ad = kkn * a
        lhs = jnp.concatenate([_stack_heads(-kkn * w_exc, lo), _stack_heads(r * w_inc, lo)], axis=0)
        rhs = jnp.concatenate([_stack_heads(ad * w_inv, lo), _stack_heads(k2 * w_inv, lo)], axis=0)
        g = _bdot_nt(lhs, rhs)
        a_ab = jnp.where(strict, g[0:2 * C, 0:2 * C], 0.0)
        a_ak = jnp.where(strict, g[0:2 * C, 2 * C:4 * C], 0.0)
        b_ab = jnp.where(incl, g[2 * C:4 * C, 0:2 * C], 0.0)
        b_ak = jnp.where(incl, g[2 * C:4 * C, 2 * C:4 * C], 0.0)
        s = s_scr[p]
        ps = _bdot_nt(lhs, s)
        vs = _stack_heads(vv, lo)
        x = ps[0:2 * C] + _bdot(a_ak, vs)
        apow = a_ab
        n_sq = C.bit_length() - 1
        for i in range(n_sq):
            x = x + _bdot(apow, x)
            if i + 1 < n_sq:
                apow = _bdot(apow, apow)
        uv = jnp.concatenate([x, vs], axis=0)
        ys = ps[2 * C:4 * C] + _bdot(jnp.concatenate([b_ab, b_ak], axis=1), uv)
        y = ys[0:C] + ys[C:2 * C]
        ends = jnp.concatenate([_stack_heads(ad * w_end, lo), _stack_heads(k2 * w_end, lo)], axis=0)
        s_scr[p] = s * w_all + _bdot_tn(uv, ends)

        mean = _head_sum(y, lo) * (1.0 / HEAD_DIM)
        d = y - mean
        var = _head_sum(d * d, lo) * (1.0 / HEAD_DIM)
        yn = d * lax.rsqrt(var + GN_EPS) * lnw_ref[:, sl] + lnb_ref[:, sl]
        bonus = _head_sum(r * k2 * rkw_ref[:, sl], lo)
        oa_ref[:, sl] = ((yn + bonus * v) * _silu(ga_ref[:, sl])).astype(oa_ref.dtype)

    @pl.when(t == n_t - 1)
    def _():
        st_ref[0] = s_scr[...]
        sht_ref[0] = ua_ref[last_valid - 1:last_valid, :]


def _rwkv(ua, ga, s0_bd, sh0, prm, tri, B, T, t_valid):
    n_t = T // CHUNK
    last_valid = t_valid - (n_t - 1) * CHUNK
    assert 1 <= last_valid <= CHUNK
    row = lambda b, t: (b * n_t + t, 0)
    const = lambda b, t: (0, 0)
    vec = lambda w: pl.BlockSpec((1, w), const)
    kern = functools.partial(_rwkv_kernel, n_t=n_t, last_valid=last_valid)
    return pl.pallas_call(
        kern,
        grid=(B, n_t),
        in_specs=[pl.BlockSpec((CHUNK, W_SHIFT), row),
                  pl.BlockSpec((CHUNK, D_A), row),
                  pl.BlockSpec((1, N_PAIR_A, LANES, LANES), lambda b, t: (b, 0, 0, 0)),
                  pl.BlockSpec((1, 1, W_SHIFT), lambda b, t: (b, 0, 0)),
                  vec(W_SHIFT), vec(D_A), pl.BlockSpec((R_W, D_A), const),
                  vec(D_A), pl.BlockSpec((R_A, D_A), const),
                  vec(D_A), vec(D_A), vec(D_A), vec(D_A), vec(D_A),
                  pl.BlockSpec((CHUNK, CHUNK), const)],
        out_specs=[pl.BlockSpec((CHUNK, D_A), row),
                   pl.BlockSpec((1, N_PAIR_A, LANES, LANES), lambda b, t: (b, 0, 0, 0)),
                   pl.BlockSpec((1, 1, W_SHIFT), lambda b, t: (b, 0, 0))],
        out_shape=[jax.ShapeDtypeStruct((B * T, D_A), BF16),
                   jax.ShapeDtypeStruct((B, N_PAIR_A, LANES, LANES), F32),
                   jax.ShapeDtypeStruct((B, 1, W_SHIFT), F32)],
        scratch_shapes=[pltpu.VMEM((N_PAIR_A, LANES, LANES), F32),
                        pltpu.VMEM((1, W_SHIFT), F32),
                        pltpu.VMEM((CHUNK, W_SHIFT), F32)],
        compiler_params=_cparams(("parallel", "arbitrary")),
        name="rwkv7_chunk",
    )(ua, ga, s0_bd, sh0, prm["mu"], prm["w0"], prm["w2"], prm["a0"], prm["a2"],
      prm["k_k"], prm["k_a"], prm["r_k"], prm["ln_w"], prm["ln_b"], tri)


CONV_PAD = 8


def _ssd_kernel(xbc_ref, z_ref, dt_ref, s0_ref, conv0_ref, cw_ref, cb_ref, dtb_ref, alog_ref,
                dx_ref, nw_ref, tri_ref, e64_ref, mrow_ref, mtril_ref,
                ob_ref, st_ref, convt_ref,
                s_scr, ext_scr, *, n_t, last_valid):
    C = CHUNK
    t = pl.program_id(1)
    n_prev = CONV_W - 1

    @pl.when(t == 0)
    def _():
        for p in range(N_PAIR_B):
            s_scr[p] = s0_ref[0, p].T
        ext_scr[CONV_PAD - n_prev:CONV_PAD, :] = conv0_ref[0]

    ext_scr[CONV_PAD:CONV_PAD + C, :] = xbc_ref[...]
    conv = cb_ref[...]
    for i in range(CONV_W):
        conv = conv + ext_scr[CONV_PAD - n_prev + i:CONV_PAD - n_prev + i + C, :] * cw_ref[i:i + 1, :]

    @pl.when(t == n_t - 1)
    def _():
        convt_ref[0] = ext_scr[CONV_PAD + last_valid - n_prev:CONV_PAD + last_valid, :]

    ext_scr[CONV_PAD - n_prev:CONV_PAD, :] = ext_scr[CONV_PAD + C - n_prev:CONV_PAD + C, :]

    xa = _silu(conv)
    xs = xa[:, 0:D_B]
    bm = xa[:, D_B:D_B + N_GROUPS * D_STATE]
    cm = xa[:, D_B + N_GROUPS * D_STATE:CONV_DIM]

    dtv = _softplus(dt_ref[...] + dtb_ref[...])
    if last_valid < C:
        row = lax.broadcasted_iota(jnp.int32, (C, 1), 0)
        dtv = jnp.where(jnp.logical_or(t < n_t - 1, row < last_valid), dtv, 0.0)
    da = dtv * (-jnp.exp(alog_ref[...]))
    a_cs = _sel_dot(tri_ref[...], da)
    ex = _dot_sel(jnp.concatenate([dtv, da, a_cs], axis=0), e64_ref[...])
    dt_x, da_x, acs_x = ex[0:C], ex[C:2 * C], ex[2 * C:3 * C]
    acs_row = jnp.sum(da_x * mrow_ref[...], axis=0, keepdims=True)
    seg = jnp.where(mtril_ref[...] > 0.0, jnp.exp(acs_x - acs_row), 0.0)

    hpg = H_B // N_GROUPS
    cb_tiles = []
    for g in range(N_GROUPS):
        b_g = bm[:, g * D_STATE:(g + 1) * D_STATE]
        c_g = cm[:, g * D_STATE:(g + 1) * D_STATE]
        cb_tiles.append(_bdot_nt(c_g, jnp.concatenate([b_g] * hpg, axis=0)))
    scores = jnp.concatenate(cb_tiles, axis=1) * seg
    xdt = xs * dt_x
    acs_last = acs_x[C - 1:C, :]
    e_in = jnp.exp(acs_x)
    e_end = jnp.exp(acs_last - acs_x)
    e_all = jnp.exp(acs_last)

    lo = _lane_lo((C, LANES))
    pairs_per_group = N_PAIR_B // N_GROUPS
    ys = []
    for p in range(N_PAIR_B):
        g = p // pairs_per_group
        sl = slice(p * LANES, (p + 1) * LANES)
        b_g = bm[:, g * D_STATE:(g + 1) * D_STATE]
        c_g = cm[:, g * D_STATE:(g + 1) * D_STATE]
        xp = xdt[:, sl]
        s = s_scr[p]
        y = _bdot(scores[:, sl], _stack_heads(xp, lo))
        y = y + _bdot(c_g, s) * e_in[:, sl]
        s_scr[p] = s * e_all[:, sl] + _bdot_tn(b_g, xp * e_end[:, sl])
        y = y + dx_ref[:, sl] * xs[:, sl]
        ys.append(y * _silu(z_ref[:, sl]))

    gw = D_B // N_GROUPS
    for g in range(N_GROUPS):
        yg = jnp.concatenate(ys[g * pairs_per_group:(g + 1) * pairs_per_group], axis=1)
        ms = jnp.mean(yg * yg, axis=-1, keepdims=True)
        ob_ref[:, g * gw:(g + 1) * gw] = (
            yg * lax.rsqrt(ms + 1e-5) * nw_ref[:, g * gw:(g + 1) * gw]).astype(ob_ref.dtype)

    @pl.when(t == n_t - 1)
    def _():
        for p in range(N_PAIR_B):
            st_ref[0, p] = s_scr[p].T


def _ssd(xbc, zb, dt, s0, conv0, prm, consts, B, T, t_valid):
    n_t = T // CHUNK
    last_valid = t_valid - (n_t - 1) * CHUNK
    assert CONV_W - 1 <= last_valid <= CHUNK
    row = lambda b, t: (b * n_t + t, 0)
    const = lambda b, t: (0, 0)
    vec = lambda w: pl.BlockSpec((1, w), const)
    kern = functools.partial(_ssd_kernel, n_t=n_t, last_valid=last_valid)
    return pl.pallas_call(
        kern,
        grid=(B, n_t),
        in_specs=[pl.BlockSpec((CHUNK, CONV_DIM), row),
                  pl.BlockSpec((CHUNK, D_B), row),
                  pl.BlockSpec((CHUNK, DT_PAD), row),
                  pl.BlockSpec((1, N_PAIR_B, LANES, D_STATE), lambda b, t: (b, 0, 0, 0)),
                  pl.BlockSpec((1, CONV_W - 1, CONV_DIM), lambda b, t: (b, 0, 0)),
                  pl.BlockSpec((CONV_W, CONV_DIM), const), vec(CONV_DIM),
                  vec(DT_PAD), vec(DT_PAD), vec(D_B), vec(D_B),
                  pl.BlockSpec((CHUNK, CHUNK), const),
                  pl.BlockSpec((DT_PAD, D_B), const),
                  pl.BlockSpec((CHUNK, D_B), const),
                  pl.BlockSpec((CHUNK, D_B), const)],
        out_specs=[pl.BlockSpec((CHUNK, D_B), row),
                   pl.BlockSpec((1, N_PAIR_B, LANES, D_STATE), lambda b, t: (b, 0, 0, 0)),
                   pl.BlockSpec((1, CONV_W - 1, CONV_DIM), lambda b, t: (b, 0, 0))],
        out_shape=[jax.ShapeDtypeStruct((B * T, D_B), BF16),
                   jax.ShapeDtypeStruct((B, N_PAIR_B, LANES, D_STATE), F32),
                   jax.ShapeDtypeStruct((B, CONV_W - 1, CONV_DIM), F32)],
        scratch_shapes=[pltpu.VMEM((N_PAIR_B, D_STATE, LANES), F32),
                        pltpu.VMEM((CONV_PAD + CHUNK, CONV_DIM), F32)],
        compiler_params=_cparams(("parallel", "arbitrary")),
        name="ssd_chunk",
    )(xbc, zb, dt, s0, conv0, prm["conv_w"], prm["conv_b"], prm["dt_bias"], prm["a_log"],
      prm["d_x"], prm["norm_w"], consts["tri"], consts["e64"], consts["mrow"], consts["mtril"])


def _kv_prep_kernel(k_ref, v_ref, w_ref, kn_ref, vb_ref, knew_ref, vnew_ref):
    tm = k_ref.shape[0]
    lo = _lane_lo((tm, LANES))
    for p in range(N_PAIR_C):
        sl = slice(p * LANES, (p + 1) * LANES)
        k = k_ref[:, sl]
        ms = _head_sum(k * k, lo) * (1.0 / HEAD_DIM)
        kn = k * lax.rsqrt(ms + 1e-6) * w_ref[:, sl]
        v = v_ref[:, sl]
        kn_ref[:, sl] = kn.astype(kn_ref.dtype)
        vb_ref[:, sl] = v.astype(vb_ref.dtype)
        for j in range(PAIR):
            h = p * PAIR + j
            knew_ref[0, h] = kn[:, j * HEAD_DIM:(j + 1) * HEAD_DIM]
            vnew_ref[0, h] = v[:, j * HEAD_DIM:(j + 1) * HEAD_DIM]


def _kv_prep(k, v, knw, B, T, tm):
    n_t = T // tm
    row = lambda b, t: (b * n_t + t, 0)
    hm = pl.BlockSpec((1, H_C, tm, HEAD_DIM), lambda b, t: (b, 0, t, 0))
    return pl.pallas_call(
        _kv_prep_kernel,
        grid=(B, n_t),
        in_specs=[pl.BlockSpec((tm, D_C), row), pl.BlockSpec((tm, D_C), row),
                  pl.BlockSpec((1, D_C), lambda b, t: (0, 0))],
        out_specs=[pl.BlockSpec((tm, D_C), row), pl.BlockSpec((tm, D_C), row), hm, hm],
        out_shape=[jax.ShapeDtypeStruct((B * T, D_C), BF16),
                   jax.ShapeDtypeStruct((B * T, D_C), BF16),
                   jax.ShapeDtypeStruct((B, H_C, T, HEAD_DIM), F32),
                   jax.ShapeDtypeStruct((B, H_C, T, HEAD_DIM), F32)],
        compiler_params=_cparams(("parallel", "parallel")),
        name="sb_kv_prep",
    )(k, v, knw)


KB = LANES


def _sb_attn_kernel(q_ref, k_ref, v_ref, g_ref, qw_ref, mcat_ref, o_ref, *, tq, q_start):
    qi = pl.program_id(2)
    lo = _lane_lo((tq, LANES))
    q = q_ref[...]
    ms = _head_sum(q * q, lo) * (1.0 / HEAD_DIM)
    qn = q * lax.rsqrt(ms + 1e-6) * qw_ref[...] * SB_SCALE
    zero = jnp.zeros_like(qn)
    q_heads = (jnp.where(lo, qn, zero).astype(BF16), jnp.where(lo, zero, qn).astype(BF16))
    q0 = q_start + qi * tq
    kb_diag = q0 // KB
    lo_kv = _lane_lo((KB, LANES))
    mcat = mcat_ref[...]

    def block(kb, carry, masked):
        c_heads, acc = carry[:PAIR], carry[PAIR]
        ks = pl.multiple_of(kb * KB, KB)
        kblk = k_ref[pl.ds(ks, KB), :]
        vblk = v_ref[pl.ds(ks, KB), :]
        if masked:
            k_pos = ks + lax.broadcasted_iota(jnp.int32, (tq, KB), 1)
            q_pos = q0 + lax.broadcasted_iota(jnp.int32, (tq, KB), 0)
            causal = k_pos < q_pos
        atts, c_new = [], []
        for h in range(PAIR):
            z = _bdot_nt(q_heads[h], kblk)
            lk = -_softplus(z)
            if masked:
                lk = jnp.where(causal, lk, 0.0)
            tc = _dot_sel(lk, mcat, n=2)
            att = jnp.exp(z + lk + tc[:, 0:KB] + c_heads[h])
            if masked:
                att = jnp.where(causal, att, 0.0)
            atts.append(att.astype(BF16))
            c_new.append(c_heads[h] + tc[:, KB:2 * KB])
        vz = jnp.zeros_like(vblk)
        v_st = jnp.concatenate([jnp.where(lo_kv, vblk, vz), jnp.where(lo_kv, vz, vblk)], axis=0)
        acc = acc + jnp.dot(jnp.concatenate(atts, axis=1), v_st, preferred_element_type=F32)
        return (*c_new, acc)

    zeros = jnp.zeros((tq, LANES), F32)
    carry = block(kb_diag, (zeros,) * (PAIR + 1), True)
    carry = lax.fori_loop(0, kb_diag, lambda i, c: block(kb_diag - 1 - i, c, False), carry)
    o_ref[...] = (carry[PAIR] * _silu(g_ref[...])).astype(o_ref.dtype)


def _sb_attn(q, kn, vb, gc, qw, mcat, B, Tq, Tk, tq, q_start):
    nq = Tq // tq
    assert Tk % KB == 0 and tq <= KB and q_start % KB == 0 and KB % tq == 0
    qrow = lambda b, p, i: (b * nq + i, p)
    kv = lambda b, p, i: (b, p)
    kern = functools.partial(_sb_attn_kernel, tq=tq, q_start=q_start)
    return pl.pallas_call(
        kern,
        grid=(B, N_PAIR_C, nq),
        in_specs=[pl.BlockSpec((tq, LANES), qrow),
                  pl.BlockSpec((Tk, LANES), kv),
                  pl.BlockSpec((Tk, LANES), kv),
                  pl.BlockSpec((tq, LANES), qrow),
                  pl.BlockSpec((1, LANES), lambda b, p, i: (0, 0)),
                  pl.BlockSpec((KB, 2 * KB), lambda b, p, i: (0, 0))],
        out_specs=pl.BlockSpec((tq, LANES), qrow),
        out_shape=jax.ShapeDtypeStruct((B * Tq, D_C), BF16),
        compiler_params=_cparams(("parallel", "parallel", "arbitrary")),
        name="sb_attn",
    )(q, kn, vb, gc, qw, mcat)


def _out_proj_kernel(x_ref, oa_ref, ob_ref, oc_ref, w_ref, y_ref):
    acc = jnp.dot(oa_ref[...], w_ref[0:D_A, :], preferred_element_type=F32)
    acc = acc + jnp.dot(ob_ref[...], w_ref[D_A:D_A + D_B, :], preferred_element_type=F32)
    acc = acc + jnp.dot(oc_ref[...], w_ref[D_A + D_B:D_MIX, :], preferred_element_type=F32)
    y_ref[...] = x_ref[...] + acc


def _out_proj(x, oa, ob, oc, w_out, tm):
    n = x.shape[0]
    blk = lambda w: pl.BlockSpec((tm, w), lambda i: (i, 0))
    return pl.pallas_call(
        _out_proj_kernel,
        grid=(n // tm,),
        in_specs=[blk(D_MODEL), blk(D_A), blk(D_B), blk(D_C),
                  pl.BlockSpec((D_MIX, D_MODEL), lambda i: (0, 0))],
        out_specs=blk(D_MODEL),
        out_shape=jax.ShapeDtypeStruct((n, D_MODEL), F32),
        compiler_params=_cparams(("parallel",)),
        name="out_proj",
    )(x, oa, ob, oc, w_out)


def _constants():
    i64 = jnp.arange(CHUNK)
    lane = jnp.arange(D_B)
    kj = jnp.arange(KB)
    return {
        "tri": (i64[:, None] >= i64[None, :]).astype(BF16),
        "e64": (jnp.arange(DT_PAD)[:, None] == lane[None, :] // HEAD_DIM).astype(BF16),
        "mrow": (i64[:, None] <= lane[None, :] % CHUNK).astype(F32),
        "mtril": (lane[None, :] % CHUNK <= i64[:, None]).astype(F32),
        "mcat": jnp.concatenate([(kj[:, None] > kj[None, :]).astype(BF16),
                                 jnp.ones((KB, KB), BF16)], axis=1),
    }


def _pack_w_in(w):
    o1 = W_SHIFT
    o2 = o1 + D_A
    o3 = o2 + D_B
    o4 = o3 + CONV_DIM
    o5 = o4 + H_B
    dt_cols = jnp.pad(w[:, o4:o5], ((0, 0), (0, DT_PAD - H_B)))
    return jnp.concatenate([w[:, :o4], dt_cols, w[:, o5:]], axis=1).astype(BF16)


def _to_block_diag(s):
    b = s.shape[0]
    s = s.reshape(b, N_PAIR_A, PAIR, HEAD_DIM, HEAD_DIM)
    eye = jnp.eye(PAIR, dtype=s.dtype)
    out = s[:, :, :, :, None, :] * eye[None, None, :, None, :, None]
    return out.reshape(b, N_PAIR_A, LANES, LANES)


def _from_block_diag(sbd):
    b = sbd.shape[0]
    s = sbd.reshape(b, N_PAIR_A, PAIR, HEAD_DIM, PAIR, HEAD_DIM)
    s = jnp.stack([s[:, :, j, :, j, :] for j in range(PAIR)], axis=2)
    return s.reshape(b, H_A, HEAD_DIM, HEAD_DIM)


def _layer_params(l, norm_w, w_in, w_out, rwkv_mu, rwkv_w0, rwkv_w2, rwkv_a0, rwkv_a2, rwkv_k_k,
                  rwkv_k_a, rwkv_r_k, rwkv_ln_w, rwkv_ln_b, ssm_conv_w, ssm_conv_b, ssm_dt_bias,
                  ssm_A_log, ssm_D, ssm_norm_w, sb_q_norm_w, sb_k_norm_w):
    row = lambda x: x.reshape(1, -1)
    pad_h = lambda x: jnp.pad(x, (0, DT_PAD - H_B)).reshape(1, DT_PAD)
    return {
        "norm_w": row(norm_w[l]), "w_in": _pack_w_in(w_in[l]), "w_out": w_out[l].astype(BF16),
        "rwkv": {"mu": row(rwkv_mu[l]), "w0": row(rwkv_w0[l]), "w2": rwkv_w2[l].astype(BF16),
                 "a0": row(rwkv_a0[l]), "a2": rwkv_a2[l].astype(BF16), "k_k": row(rwkv_k_k[l]),
                 "k_a": row(rwkv_k_a[l]), "r_k": row(rwkv_r_k[l]), "ln_w": row(rwkv_ln_w[l]),
                 "ln_b": row(rwkv_ln_b[l])},
        "ssm": {"conv_w": ssm_conv_w[l], "conv_b": row(ssm_conv_b[l]),
                "dt_bias": pad_h(ssm_dt_bias[l]), "a_log": pad_h(ssm_A_log[l]),
                "d_x": row(jnp.repeat(ssm_D[l], HEAD_DIM)), "norm_w": row(ssm_norm_w[l])},
        "sb_qw": row(jnp.tile(sb_q_norm_w[l], PAIR)),
        "sb_kw": row(jnp.tile(sb_k_norm_w[l], H_C)),
    }


def _layer(x, prm, consts, B, T, t_valid, s_rwkv_bd, shift, s_ssm, conv_buf, k_past, v_past,
           tm_in, tm_kv, tm_out, tq):
    ua, ga, zb, xbc, dt, qc, kc, vc, gc = _in_proj(x, prm["norm_w"], prm["w_in"], tm_in)
    oa, s_rwkv_bd, shift = _rwkv(ua, ga, s_rwkv_bd, shift, prm["rwkv"], consts["tri"], B, T, t_valid)
    ob, s_ssm, conv_buf = _ssd(xbc, zb, dt, s_ssm, conv_buf, prm["ssm"], consts, B, T, t_valid)
    kn, vb, k_new, v_new = _kv_prep(kc, vc, prm["sb_kw"], B, T, tm_kv)
    if k_past is None:
        q_start, t_k = 0, T
    else:
        q_start = k_past.shape[1]
        t_k = -(-(q_start + T) // KB) * KB
        tail = jnp.zeros((B, t_k - q_start - T, D_C), BF16)
        cat = lambda past, new: jnp.concatenate(
            [past, new.reshape(B, T, D_C), tail], axis=1).reshape(B * t_k, D_C)
        kn, vb = cat(k_past, kn), cat(v_past, vb)
    oc = _sb_attn(qc, kn, vb, gc, prm["sb_qw"], consts["mcat"], B, T, t_k, tq, q_start)
    y = _out_proj(x, oa, ob, oc, prm["w_out"], tm_out)
    return y, (s_rwkv_bd, shift, s_ssm, conv_buf, k_new, v_new)


def _packed_cache(c):
    b, h, p, d = c.shape
    return c.transpose(0, 2, 1, 3).reshape(b, p, h * d).astype(BF16)


def kernel(x_prompt, x_sample, state_rwkv, state_rwkv_shift, state_ssm, state_conv, cache_sb_k, cache_sb_v,
           norm_w, w_in, w_out, rwkv_mu, rwkv_w0, rwkv_w2, rwkv_a0, rwkv_a2, rwkv_k_k, rwkv_k_a, rwkv_r_k,
           rwkv_ln_w, rwkv_ln_b, ssm_conv_w, ssm_conv_b, ssm_dt_bias, ssm_A_log, ssm_D, ssm_norm_w,
           sb_q_norm_w, sb_k_norm_w):
    bp, tp, _ = x_prompt.shape
    bs, ts, _ = x_sample.shape
    depth = w_in.shape[0]
    ts_pad = -(-ts // CHUNK) * CHUNK
    consts = _constants()

    yp = x_prompt.reshape(bp * tp, D_MODEL)
    ys = jnp.pad(x_sample, ((0, 0), (0, ts_pad - ts), (0, 0))).reshape(bs * ts_pad, D_MODEL)

    new_p = [[] for _ in range(6)]
    new_s = [[] for _ in range(6)]
    for l in range(depth):
        prm = _layer_params(l, norm_w, w_in, w_out, rwkv_mu, rwkv_w0, rwkv_w2, rwkv_a0, rwkv_a2,
                            rwkv_k_k, rwkv_k_a, rwkv_r_k, rwkv_ln_w, rwkv_ln_b, ssm_conv_w, ssm_conv_b,
                            ssm_dt_bias, ssm_A_log, ssm_D, ssm_norm_w, sb_q_norm_w, sb_k_norm_w)
        yp, st_p = _layer(
            yp, prm, consts, bp, tp, tp,
            jnp.zeros((bp, N_PAIR_A, LANES, LANES), F32), jnp.zeros((bp, 1, W_SHIFT), F32),
            jnp.zeros((bp, N_PAIR_B, LANES, D_STATE), F32), jnp.zeros((bp, CONV_W - 1, CONV_DIM), F32),
            None, None, tm_in=256, tm_kv=256, tm_out=512, tq=KB)
        ys, st_s = _layer(
            ys, prm, consts, bs, ts_pad, ts,
            _to_block_diag(state_rwkv[l]), state_rwkv_shift[l],
            state_ssm[l].reshape(bs, N_PAIR_B, LANES, D_STATE), state_conv[l],
            _packed_cache(cache_sb_k[l]), _packed_cache(cache_sb_v[l]),
            tm_in=256, tm_kv=ts_pad, tm_out=512, tq=ts_pad)
        for i in range(6):
            new_p[i].append(st_p[i])
            new_s[i].append(st_s[i])

    def finish(y, st, b, t_pad, t):
        s_rwkv, shift, s_ssm, conv, k_new, v_new = [jnp.stack(v) for v in st]
        return (y.reshape(b, t_pad, D_MODEL)[:, :t],
                (_from_block_diag(s_rwkv.reshape(depth * b, N_PAIR_A, LANES, LANES))
                 .reshape(depth, b, H_A, HEAD_DIM, HEAD_DIM)),
                shift,
                s_ssm.reshape(depth, b, H_B, HEAD_DIM, D_STATE),
                conv,
                k_new[:, :, :, :t],
                v_new[:, :, :, :t])

    yp, *rest_p = finish(yp, new_p, bp, tp, tp)
    ys, *rest_s = finish(ys, new_s, bs, ts_pad, ts)
    return (yp, ys, *rest_p, *rest_s)
```

```python
import functools

import jax
import jax.numpy as jnp
from jax import lax
from jax.experimental import pallas as pl
from jax.experimental.pallas import tpu as pltpu

F32 = jnp.float32
BF16 = jnp.bfloat16

D_MODEL = 1024
D_MIX = 2 * D_MODEL
HEAD_DIM = 64
D_A = 768
H_A = D_A // HEAD_DIM
R_W = 64
R_A = 64
GN_EPS = 64e-5
D_B = 768
H_B = D_B // HEAD_DIM
N_GROUPS = 2
D_STATE = 128
CONV_W = 4
CONV_DIM = D_B + 2 * N_GROUPS * D_STATE
D_C = 512
H_C = D_C // HEAD_DIM
W_SHIFT = 3 * D_A + R_W + R_A
SB_SCALE = HEAD_DIM ** -0.5

LANES = 128
PAIR = LANES // HEAD_DIM
CHUNK = 64
VMEM_LIMIT = 56 * 1024 * 1024

N_PAIR_A = H_A // PAIR
N_PAIR_B = H_B // PAIR
N_PAIR_C = H_C // PAIR
DT_PAD = LANES

IN_WIDTHS = (W_SHIFT, D_A, D_B, CONV_DIM, DT_PAD, D_C, D_C, D_C, D_C)


def _cparams(sem):
    return pltpu.CompilerParams(dimension_semantics=sem, vmem_limit_bytes=VMEM_LIMIT)


def _bdot(a, b):
    return jnp.dot(a.astype(BF16), b.astype(BF16), preferred_element_type=F32)


def _bdot_nt(a, b):
    return lax.dot_general(a.astype(BF16), b.astype(BF16), (((1,), (1,)), ((), ())),
                           preferred_element_type=F32)


def _bdot_tn(a, b):
    return lax.dot_general(a.astype(BF16), b.astype(BF16), (((0,), (0,)), ((), ())),
                           preferred_element_type=F32)


def _split_bf16(x, n):
    parts, r = [], x
    for i in range(n):
        p = r.astype(BF16)
        parts.append(p)
        if i + 1 < n:
            r = r - p.astype(F32)
    return parts


def _sel_dot(sel, x, n=3):
    acc = None
    for p in _split_bf16(x, n):
        d = jnp.dot(sel, p, preferred_element_type=F32)
        acc = d if acc is None else acc + d
    return acc


def _dot_sel(x, sel, n=3):
    acc = None
    for p in _split_bf16(x, n):
        d = jnp.dot(p, sel, preferred_element_type=F32)
        acc = d if acc is None else acc + d
    return acc


def _lane_lo(shape):
    return lax.broadcasted_iota(jnp.int32, shape, len(shape) - 1) < HEAD_DIM


def _head_sum(x, lo):
    s_lo = jnp.sum(jnp.where(lo, x, 0.0), axis=-1, keepdims=True)
    s_hi = jnp.sum(jnp.where(lo, 0.0, x), axis=-1, keepdims=True)
    return jnp.where(lo, s_lo, s_hi)


def _stack_heads(x, lo):
    zero = jnp.zeros_like(x)
    return jnp.concatenate([jnp.where(lo, x, zero), jnp.where(lo, zero, x)], axis=0)


def _silu(x):
    return x * jax.nn.sigmoid(x)


def _softplus(x):
    return jnp.maximum(x, 0.0) + jnp.log1p(jnp.exp(-jnp.abs(x)))


def _softplus_abs(x):
    return jnp.maximum(x, 0.0) + jnp.log(1.0 + jnp.exp(-jnp.abs(x)))


def _in_proj_kernel(x_ref, nw_ref, w_ref, *out_refs):
    x = x_ref[...]
    ms = jnp.mean(x * x, axis=-1, keepdims=True)
    h = (x * lax.rsqrt(ms + 1e-6) * nw_ref[...]).astype(BF16)
    off = 0
    for o_ref, wd in zip(out_refs, IN_WIDTHS):
        o_ref[...] = jnp.dot(h, w_ref[:, off:off + wd], preferred_element_type=F32)
        off += wd


def _in_proj(x, norm_w, w_cat, tm):
    n = x.shape[0]
    n_cols = sum(IN_WIDTHS)
    return pl.pallas_call(
        _in_proj_kernel,
        grid=(n // tm,),
        in_specs=[pl.BlockSpec((tm, D_MODEL), lambda i: (i, 0)),
                  pl.BlockSpec((1, D_MODEL), lambda i: (0, 0)),
                  pl.BlockSpec((D_MODEL, n_cols), lambda i: (0, 0))],
        out_specs=[pl.BlockSpec((tm, wd), lambda i: (i, 0)) for wd in IN_WIDTHS],
        out_shape=[jax.ShapeDtypeStruct((n, wd), F32) for wd in IN_WIDTHS],
        compiler_params=_cparams(("parallel",)),
        name="in_proj",
    )(x, norm_w, w_cat)


def _rwkv_kernel(ua_ref, ga_ref, s0_ref, sh0_ref, mu_ref, w0_ref, w2_ref, a0_ref, a2_ref,
                 kkw_ref, kaw_ref, rkw_ref, lnw_ref, lnb_ref, tri_ref,
                 oa_ref, st_ref, sht_ref,
                 s_scr, prev_scr, us_scr, *, n_t, last_valid):
    C = CHUNK
    t = pl.program_id(1)

    @pl.when(t == 0)
    def _():
        s_scr[...] = s0_ref[0]
        prev_scr[...] = sh0_ref[0]

    u = ua_ref[...]
    row = lax.broadcasted_iota(jnp.int32, (C, 1), 0)
    u_prev = jnp.where(row == 0, prev_scr[...], pltpu.roll(u, 1, axis=0))
    us_scr[...] = u + (u_prev - u) * mu_ref[...]
    prev_scr[...] = u[C - 1:C, :]

    padded = last_valid < C
    valid = jnp.logical_or(t < n_t - 1, row < last_valid) if padded else None

    w_lo = us_scr[:, 3 * D_A:3 * D_A + R_W]
    a_lo = us_scr[:, 3 * D_A + R_W:W_SHIFT]
    wl = w0_ref[...] + _bdot(jnp.tanh(w_lo), w2_ref[...])
    lw = -jnp.exp(-_softplus(-wl) - 0.5)
    a_all = jax.nn.sigmoid(a0_ref[...] + _bdot(a_lo, a2_ref[...]))
    if padded:
        lw = jnp.where(valid, lw, 0.0)
    cum_all = _sel_dot(tri_ref[...], lw)

    lo = _lane_lo((C, LANES))
    ri = lax.broadcasted_iota(jnp.int32, (2 * C, 2 * C), 0) & (C - 1)
    ci = lax.broadcasted_iota(jnp.int32, (2 * C, 2 * C), 1) & (C - 1)
    strict = ci < ri
    incl = ci <= ri

    pairs = range(N_PAIR_A)
    lanes = [slice(p * LANES, (p + 1) * LANES) for p in pairs]
    lhs, rhs, ends, vs, w_all, extra = [], [], [], [], [], []
    for p in pairs:
        sl = lanes[p]
        r = us_scr[:, p * LANES:(p + 1) * LANES]
        k = us_scr[:, D_A + p * LANES:D_A + (p + 1) * LANES]
        v = us_scr[:, 2 * D_A + p * LANES:2 * D_A + (p + 1) * LANES]
        a = a_all[:, sl]
        cum = cum_all[:, sl]
        kk = k * kkw_ref[:, sl]
        kkn = kk / jnp.maximum(jnp.sqrt(_head_sum(kk * kk, lo)), 1e-12)
        k2 = k * (1.0 + (a - 1.0) * kaw_ref[:, sl])
        vv = v
        if padded:
            kkn = jnp.where(valid, kkn, 0.0)
            k2 = jnp.where(valid, k2, 0.0)
            vv = jnp.where(valid, v, 0.0)
        cum_last = cum[C - 1:C, :]
        w_inc = jnp.exp(cum)
        w_exc = jnp.exp(cum - lw[:, sl])
        w_inv = jnp.exp(-cum)
        w_end = jnp.exp(cum_last - cum)
        w_all.append(jnp.exp(cum_last))
        ad = kkn * a
        lhs.append(jnp.concatenate(
            [_stack_heads(-kkn * w_exc, lo), _stack_heads(r * w_inc, lo)], axis=0).astype(BF16))
        rhs.append(jnp.concatenate(
            [_stack_heads(ad * w_inv, lo), _stack_heads(k2 * w_inv, lo)], axis=0).astype(BF16))
        ends.append(jnp.concatenate(
            [_stack_heads(ad * w_end, lo), _stack_heads(k2 * w_end, lo)], axis=0).astype(BF16))
        vs.append(_stack_heads(vv, lo).astype(BF16))
        bonus = _head_sum(r * k2 * rkw_ref[:, sl], lo)
        extra.append((bonus * v, _silu(ga_ref[:, sl])))

    g = [_bdot_nt(lhs[p], rhs[p]) for p in pairs]
    ps = [_bdot_nt(lhs[p], s_scr[p]) for p in pairs]
    x = [ps[p][0:2 * C] + _bdot(jnp.where(strict, g[p][0:2 * C, 2 * C:4 * C], 0.0), vs[p])
         for p in pairs]
    apow = [jnp.where(strict, g[p][0:2 * C, 0:2 * C], 0.0).astype(BF16) for p in pairs]
    n_sq = C.bit_length() - 1
    for i in range(n_sq):
        x = [x[p] + _bdot(apow[p], x[p]) for p in pairs]
        if i + 1 < n_sq:
            apow = [_bdot(apow[p], apow[p]).astype(BF16) for p in pairs]
    for p in pairs:
        sl = lanes[p]
        b_ab = jnp.where(incl, g[p][2 * C:4 * C, 0:2 * C], 0.0).astype(BF16)
        b_ak = jnp.where(incl, g[p][2 * C:4 * C, 2 * C:4 * C], 0.0).astype(BF16)
        uv = jnp.concatenate([x[p].astype(BF16), vs[p]], axis=0)
        ys = ps[p][2 * C:4 * C] + _bdot(jnp.concatenate([b_ab, b_ak], axis=1), uv)
        y = ys[0:C] + ys[C:2 * C]
        s_scr[p] = s_scr[p] * w_all[p] + _bdot_tn(uv, ends[p])

        mean = _head_sum(y, lo) * (1.0 / HEAD_DIM)
        d = y - mean
        var = _head_sum(d * d, lo) * (1.0 / HEAD_DIM)
        yn = d * lax.rsqrt(var + GN_EPS) * lnw_ref[:, sl] + lnb_ref[:, sl]
        bv, gate = extra[p]
        oa_ref[:, sl] = ((yn + bv) * gate).astype(oa_ref.dtype)

    @pl.when(t == n_t - 1)
    def _():
        st_ref[0] = s_scr[...]
        sht_ref[0] = ua_ref[last_valid - 1:last_valid, :]


def _rwkv(ua, ga, s0_bd, sh0, prm, tri, B, T, t_valid):
    n_t = T // CHUNK
    last_valid = t_valid - (n_t - 1) * CHUNK
    assert 1 <= last_valid <= CHUNK
    row = lambda b, t: (b * n_t + t, 0)
    const = lambda b, t: (0, 0)
    vec = lambda w: pl.BlockSpec((1, w), const)
    kern = functools.partial(_rwkv_kernel, n_t=n_t, last_valid=last_valid)
    return pl.pallas_call(
        kern,
        grid=(B, n_t),
        in_specs=[pl.BlockSpec((CHUNK, W_SHIFT), row),
                  pl.BlockSpec((CHUNK, D_A), row),
                  pl.BlockSpec((1, N_PAIR_A, LANES, LANES), lambda b, t: (b, 0, 0, 0)),
                  pl.BlockSpec((1, 1, W_SHIFT), lambda b, t: (b, 0, 0)),
                  vec(W_SHIFT), vec(D_A), pl.BlockSpec((R_W, D_A), const),
                  vec(D_A), pl.BlockSpec((R_A, D_A), const),
                  vec(D_A), vec(D_A), vec(D_A), vec(D_A), vec(D_A),
                  pl.BlockSpec((CHUNK, CHUNK), const)],
        out_specs=[pl.BlockSpec((CHUNK, D_A), row),
                   pl.BlockSpec((1, N_PAIR_A, LANES, LANES), lambda b, t: (b, 0, 0, 0)),
                   pl.BlockSpec((1, 1, W_SHIFT), lambda b, t: (b, 0, 0))],
        out_shape=[jax.ShapeDtypeStruct((B * T, D_A), BF16),
                   jax.ShapeDtypeStruct((B, N_PAIR_A, LANES, LANES), F32),
                   jax.ShapeDtypeStruct((B, 1, W_SHIFT), F32)],
        scratch_shapes=[pltpu.VMEM((N_PAIR_A, LANES, LANES), F32),
                        pltpu.VMEM((1, W_SHIFT), F32),
                        pltpu.VMEM((CHUNK, W_SHIFT), F32)],
        compiler_params=_cparams(("parallel", "arbitrary")),
        name="rwkv7_chunk",
    )(ua, ga, s0_bd, sh0, prm["mu"], prm["w0"], prm["w2"], prm["a0"], prm["a2"],
      prm["k_k"], prm["k_a"], prm["r_k"], prm["ln_w"], prm["ln_b"], tri)


CONV_PAD = 8


def _ssd_kernel(xbc_ref, z_ref, dt_ref, s0_ref, conv0_ref, cw_ref, cb_ref, dtb_ref, alog_ref,
                dx_ref, nw_ref, tri_ref, e64_ref, mrow_ref, mtril_ref,
                ob_ref, st_ref, convt_ref,
                s_scr, ext_scr, *, n_t, last_valid):
    C = CHUNK
    t = pl.program_id(1)
    n_prev = CONV_W - 1

    @pl.when(t == 0)
    def _():
        for p in range(N_PAIR_B):
            s_scr[p] = s0_ref[0, p].T
        ext_scr[CONV_PAD - n_prev:CONV_PAD, :] = conv0_ref[0]

    ext_scr[CONV_PAD:CONV_PAD + C, :] = xbc_ref[...]
    conv = cb_ref[...]
    for i in range(CONV_W):
        conv = conv + ext_scr[CONV_PAD - n_prev + i:CONV_PAD - n_prev + i + C, :] * cw_ref[i:i + 1, :]

    @pl.when(t == n_t - 1)
    def _():
        convt_ref[0] = ext_scr[CONV_PAD + last_valid - n_prev:CONV_PAD + last_valid, :]

    ext_scr[CONV_PAD - n_prev:CONV_PAD, :] = ext_scr[CONV_PAD + C - n_prev:CONV_PAD + C, :]

    xa = _silu(conv)
    xs = xa[:, 0:D_B]
    bm = xa[:, D_B:D_B + N_GROUPS * D_STATE]
    cm = xa[:, D_B + N_GROUPS * D_STATE:CONV_DIM]

    dtv = _softplus(dt_ref[...] + dtb_ref[...])
    if last_valid < C:
        row = lax.broadcasted_iota(jnp.int32, (C, 1), 0)
        dtv = jnp.where(jnp.logical_or(t < n_t - 1, row < last_valid), dtv, 0.0)
    da = dtv * (-jnp.exp(alog_ref[...]))
    a_cs = _sel_dot(tri_ref[...], da)
    ex = _dot_sel(jnp.concatenate([dtv, da, a_cs], axis=0), e64_ref[...])
    dt_x, da_x, acs_x = ex[0:C], ex[C:2 * C], ex[2 * C:3 * C]
    acs_row = jnp.sum(da_x * mrow_ref[...], axis=0, keepdims=True)
    seg = jnp.where(mtril_ref[...] > 0.0, jnp.exp(acs_x - acs_row), 0.0)

    hpg = H_B // N_GROUPS
    cb_tiles = []
    for g in range(N_GROUPS):
        b_g = bm[:, g * D_STATE:(g + 1) * D_STATE]
        c_g = cm[:, g * D_STATE:(g + 1) * D_STATE]
        cb_tiles.append(_bdot_nt(c_g, jnp.concatenate([b_g] * hpg, axis=0)))
    scores = jnp.concatenate(cb_tiles, axis=1) * seg
    xdt = xs * dt_x
    acs_last = acs_x[C - 1:C, :]
    e_in = jnp.exp(acs_x)
    e_end = jnp.exp(acs_last - acs_x)
    e_all = jnp.exp(acs_last)

    lo = _lane_lo((C, LANES))
    pairs_per_group = N_PAIR_B // N_GROUPS
    ys = []
    for p in range(N_PAIR_B):
        g = p // pairs_per_group
        sl = slice(p * LANES, (p + 1) * LANES)
        b_g = bm[:, g * D_STATE:(g + 1) * D_STATE]
        c_g = cm[:, g * D_STATE:(g + 1) * D_STATE]
        xp = xdt[:, sl]
        s = s_scr[p]
        y = _bdot(scores[:, sl], _stack_heads(xp, lo))
        y = y + _bdot(c_g, s) * e_in[:, sl]
        s_scr[p] = s * e_all[:, sl] + _bdot_tn(b_g, xp * e_end[:, sl])
        y = y + dx_ref[:, sl] * xs[:, sl]
        ys.append(y * _silu(z_ref[:, sl]))

    gw = D_B // N_GROUPS
    for g in range(N_GROUPS):
        yg = jnp.concatenate(ys[g * pairs_per_group:(g + 1) * pairs_per_group], axis=1)
        ms = jnp.mean(yg * yg, axis=-1, keepdims=True)
        ob_ref[:, g * gw:(g + 1) * gw] = (
            yg * lax.rsqrt(ms + 1e-5) * nw_ref[:, g * gw:(g + 1) * gw]).astype(ob_ref.dtype)

    @pl.when(t == n_t - 1)
    def _():
        for p in range(N_PAIR_B):
            st_ref[0, p] = s_scr[p].T


def _ssd(xbc, zb, dt, s0, conv0, prm, consts, B, T, t_valid):
    n_t = T // CHUNK
    last_valid = t_valid - (n_t - 1) * CHUNK
    assert CONV_W - 1 <= last_valid <= CHUNK
    row = lambda b, t: (b * n_t + t, 0)
    const = lambda b, t: (0, 0)
    vec = lambda w: pl.BlockSpec((1, w), const)
    kern = functools.partial(_ssd_kernel, n_t=n_t, last_valid=last_valid)
    return pl.pallas_call(
        kern,
        grid=(B, n_t),
        in_specs=[pl.BlockSpec((CHUNK, CONV_DIM), row),
                  pl.BlockSpec((CHUNK, D_B), row),
                  pl.BlockSpec((CHUNK, DT_PAD), row),
                  pl.BlockSpec((1, N_PAIR_B, LANES, D_STATE), lambda b, t: (b, 0, 0, 0)),
                  pl.BlockSpec((1, CONV_W - 1, CONV_DIM), lambda b, t: (b, 0, 0)),
                  pl.BlockSpec((CONV_W, CONV_DIM), const), vec(CONV_DIM),
                  vec(DT_PAD), vec(DT_PAD), vec(D_B), vec(D_B),
                  pl.BlockSpec((CHUNK, CHUNK), const),
                  pl.BlockSpec((DT_PAD, D_B), const),
                  pl.BlockSpec((CHUNK, D_B), const),
                  pl.BlockSpec((CHUNK, D_B), const)],
        out_specs=[pl.BlockSpec((CHUNK, D_B), row),
                   pl.BlockSpec((1, N_PAIR_B, LANES, D_STATE), lambda b, t: (b, 0, 0, 0)),
                   pl.BlockSpec((1, CONV_W - 1, CONV_DIM), lambda b, t: (b, 0, 0))],
        out_shape=[jax.ShapeDtypeStruct((B * T, D_B), BF16),
                   jax.ShapeDtypeStruct((B, N_PAIR_B, LANES, D_STATE), F32),
                   jax.ShapeDtypeStruct((B, CONV_W - 1, CONV_DIM), F32)],
        scratch_shapes=[pltpu.VMEM((N_PAIR_B, D_STATE, LANES), F32),
                        pltpu.VMEM((CONV_PAD + CHUNK, CONV_DIM), F32)],
        compiler_params=_cparams(("parallel", "arbitrary")),
        name="ssd_chunk",
    )(xbc, zb, dt, s0, conv0, prm["conv_w"], prm["conv_b"], prm["dt_bias"], prm["a_log"],
      prm["d_x"], prm["norm_w"], consts["tri"], consts["e64"], consts["mrow"], consts["mtril"])


def _kv_prep_kernel(k_ref, v_ref, w_ref, *refs):
    kn_ref, vb_ref, knew_ref, vnew_ref = refs[-4:]
    tm = k_ref.shape[0]
    lo = _lane_lo((tm, LANES))
    for p in range(N_PAIR_C):
        sl = slice(p * LANES, (p + 1) * LANES)
        k = k_ref[:, sl]
        ms = _head_sum(k * k, lo) * (1.0 / HEAD_DIM)
        kn = k * lax.rsqrt(ms + 1e-6) * w_ref[:, sl]
        v = v_ref[:, sl]
        kn_ref[:, sl] = kn.astype(kn_ref.dtype)
        vb_ref[:, sl] = v.astype(vb_ref.dtype)
        for j in range(PAIR):
            h = p * PAIR + j
            knew_ref[0, 0, h] = kn[:, j * HEAD_DIM:(j + 1) * HEAD_DIM]
            vnew_ref[0, 0, h] = v[:, j * HEAD_DIM:(j + 1) * HEAD_DIM]


def _kv_prep(k, v, knw, B, T, tm, layer, depth, kv_all):
    n_t = T // tm
    row = lambda b, t: (b * n_t + t, 0)
    hm = pl.BlockSpec((1, 1, H_C, tm, HEAD_DIM), lambda b, t: (layer, b, 0, t, 0))
    hm_shape = jax.ShapeDtypeStruct((depth, B, H_C, T, HEAD_DIM), F32)
    in_specs = [pl.BlockSpec((tm, D_C), row), pl.BlockSpec((tm, D_C), row),
                pl.BlockSpec((1, D_C), lambda b, t: (0, 0))]
    args, aliases = [k, v, knw], {}
    if kv_all is not None:
        in_specs += [pl.BlockSpec(memory_space=pl.ANY)] * 2
        args += list(kv_all)
        aliases = {3: 2, 4: 3}
    return pl.pallas_call(
        _kv_prep_kernel,
        grid=(B, n_t),
        in_specs=in_specs,
        out_specs=[pl.BlockSpec((tm, D_C), row), pl.BlockSpec((tm, D_C), row), hm, hm],
        out_shape=[jax.ShapeDtypeStruct((B * T, D_C), BF16),
                   jax.ShapeDtypeStruct((B * T, D_C), BF16), hm_shape, hm_shape],
        input_output_aliases=aliases,
        compiler_params=_cparams(("parallel", "parallel")),
        name="sb_kv_prep",
    )(*args)


KB = LANES


KG = 4


def _sb_attn_kernel(q_ref, k_ref, v_ref, g_ref, qw_ref, mcat_ref, o_ref, *, tq, q_start):
    qi = pl.program_id(2)
    gk = KG * KB
    lo = _lane_lo((tq, LANES))
    q = q_ref[...]
    ms = _head_sum(q * q, lo) * (1.0 / HEAD_DIM)
    qn = q * lax.rsqrt(ms + 1e-6) * qw_ref[...] * SB_SCALE
    zero = jnp.zeros_like(qn)
    q_heads = (jnp.where(lo, qn, zero).astype(BF16), jnp.where(lo, zero, qn).astype(BF16))
    q0 = q_start + qi * tq
    g_top = q0 // gk
    lo_kv = _lane_lo((gk, LANES))

    def group(gi, carry, masked):
        c_heads, acc = list(carry[:PAIR]), carry[PAIR]
        ks = pl.multiple_of(gi * gk, gk)
        kg = k_ref[pl.ds(ks, gk), :]
        vg = v_ref[pl.ds(ks, gk), :]
        if masked:
            k_pos = ks + lax.broadcasted_iota(jnp.int32, (tq, gk), 1)
            q_pos = q0 + lax.broadcasted_iota(jnp.int32, (tq, gk), 0)
            causal = k_pos < q_pos
        ds, splits = [], []
        for h in range(PAIR):
            z = _bdot_nt(q_heads[h], kg)
            sp = _softplus_abs(z)
            ds.append(z - sp)
            if masked:
                sp = jnp.where(causal, sp, 0.0)
            for u in range(KG):
                hi, lo_part = _split_bf16(sp[:, u * KB:(u + 1) * KB], 2)
                splits.append(jnp.concatenate([hi, lo_part], axis=1))
        tc = jnp.dot(jnp.concatenate(splits, axis=0), mcat_ref[...], preferred_element_type=F32)
        atts = []
        for h in range(PAIR):
            c = c_heads[h]
            att_h = [None] * KG
            for u in reversed(range(KG)):
                t0 = (h * KG + u) * tq
                blk = slice(u * KB, (u + 1) * KB)
                att = jnp.exp(ds[h][:, blk] + (tc[t0:t0 + tq, 0:KB] + c))
                if masked:
                    att = jnp.where(causal[:, blk], att, 0.0)
                att_h[u] = att.astype(BF16)
                c = c + tc[t0:t0 + tq, KB:2 * KB]
            c_heads[h] = c
            atts.extend(att_h)
        vz = jnp.zeros_like(vg)
        v_st = jnp.concatenate([jnp.where(lo_kv, vg, vz), jnp.where(lo_kv, vz, vg)], axis=0)
        acc = acc + jnp.dot(jnp.concatenate(atts, axis=1), v_st, preferred_element_type=F32)
        return (*c_heads, acc)

    zeros = jnp.zeros((tq, LANES), F32)
    carry = group(g_top, (zeros,) * (PAIR + 1), True)
    carry = lax.fori_loop(0, g_top, lambda i, c: group(g_top - 1 - i, c, False), carry)
    o_ref[...] = (carry[PAIR] * _silu(g_ref[...])).astype(o_ref.dtype)


def _sb_attn(q, kn, vb, gc, qw, mcat, B, Tq, Tk, tq, q_start):
    nq = Tq // tq
    gk = KG * KB
    assert Tk % gk == 0 and gk % tq == 0 and q_start % gk == 0
    qrow = lambda b, p, i: (b * nq + i, p)
    kv = lambda b, p, i: (b, p)
    kern = functools.partial(_sb_attn_kernel, tq=tq, q_start=q_start)
    return pl.pallas_call(
        kern,
        grid=(B, N_PAIR_C, nq),
        in_specs=[pl.BlockSpec((tq, LANES), qrow),
                  pl.BlockSpec((Tk, LANES), kv),
                  pl.BlockSpec((Tk, LANES), kv),
                  pl.BlockSpec((tq, LANES), qrow),
                  pl.BlockSpec((1, LANES), lambda b, p, i: (0, 0)),
                  pl.BlockSpec((2 * KB, 2 * KB), lambda b, p, i: (0, 0))],
        out_specs=pl.BlockSpec((tq, LANES), qrow),
        out_shape=jax.ShapeDtypeStruct((B * Tq, D_C), BF16),
        compiler_params=_cparams(("parallel", "parallel", "arbitrary")),
        name="sb_attn",
    )(q, kn, vb, gc, qw, mcat)


def _out_proj_kernel(x_ref, oa_ref, ob_ref, oc_ref, w_ref, y_ref):
    acc = jnp.dot(oa_ref[...], w_ref[0:D_A, :], preferred_element_type=F32)
    acc = acc + jnp.dot(ob_ref[...], w_ref[D_A:D_A + D_B, :], preferred_element_type=F32)
    acc = acc + jnp.dot(oc_ref[...], w_ref[D_A + D_B:D_MIX, :], preferred_element_type=F32)
    y_ref[...] = x_ref[...] + acc


def _out_proj(x, oa, ob, oc, w_out, tm):
    n = x.shape[0]
    blk = lambda w: pl.BlockSpec((tm, w), lambda i: (i, 0))
    return pl.pallas_call(
        _out_proj_kernel,
        grid=(n // tm,),
        in_specs=[blk(D_MODEL), blk(D_A), blk(D_B), blk(D_C),
                  pl.BlockSpec((D_MIX, D_MODEL), lambda i: (0, 0))],
        out_specs=blk(D_MODEL),
        out_shape=jax.ShapeDtypeStruct((n, D_MODEL), F32),
        compiler_params=_cparams(("parallel",)),
        name="out_proj",
    )(x, oa, ob, oc, w_out)


def _constants():
    i64 = jnp.arange(CHUNK)
    lane = jnp.arange(D_B)
    kj = jnp.arange(KB)
    return {
        "tri": (i64[:, None] >= i64[None, :]).astype(BF16),
        "e64": (jnp.arange(DT_PAD)[:, None] == lane[None, :] // HEAD_DIM).astype(BF16),
        "mrow": (i64[:, None] <= lane[None, :] % CHUNK).astype(F32),
        "mtril": (lane[None, :] % CHUNK <= i64[:, None]).astype(F32),
        "mcat": -jnp.tile(jnp.concatenate([(kj[:, None] > kj[None, :]).astype(BF16),
                                           jnp.ones((KB, KB), BF16)], axis=1), (2, 1)),
    }


def _pack_w_in(w):
    o1 = W_SHIFT
    o2 = o1 + D_A
    o3 = o2 + D_B
    o4 = o3 + CONV_DIM
    o5 = o4 + H_B
    dt_cols = jnp.pad(w[:, o4:o5], ((0, 0), (0, DT_PAD - H_B)))
    return jnp.concatenate([w[:, :o4], dt_cols, w[:, o5:]], axis=1).astype(BF16)


def _to_block_diag(s):
    b = s.shape[0]
    s = s.reshape(b, N_PAIR_A, PAIR, HEAD_DIM, HEAD_DIM)
    eye = jnp.eye(PAIR, dtype=s.dtype)
    out = s[:, :, :, :, None, :] * eye[None, None, :, None, :, None]
    return out.reshape(b, N_PAIR_A, LANES, LANES)


def _from_block_diag(sbd):
    b = sbd.shape[0]
    s = sbd.reshape(b, N_PAIR_A, PAIR, HEAD_DIM, PAIR, HEAD_DIM)
    s = jnp.stack([s[:, :, j, :, j, :] for j in range(PAIR)], axis=2)
    return s.reshape(b, H_A, HEAD_DIM, HEAD_DIM)


def _layer_params(l, norm_w, w_in, w_out, rwkv_mu, rwkv_w0, rwkv_w2, rwkv_a0, rwkv_a2, rwkv_k_k,
                  rwkv_k_a, rwkv_r_k, rwkv_ln_w, rwkv_ln_b, ssm_conv_w, ssm_conv_b, ssm_dt_bias,
                  ssm_A_log, ssm_D, ssm_norm_w, sb_q_norm_w, sb_k_norm_w):
    row = lambda x: x.reshape(1, -1)
    pad_h = lambda x: jnp.pad(x, (0, DT_PAD - H_B)).reshape(1, DT_PAD)
    return {
        "norm_w": row(norm_w[l]), "w_in": _pack_w_in(w_in[l]), "w_out": w_out[l].astype(BF16),
        "rwkv": {"mu": row(rwkv_mu[l]), "w0": row(rwkv_w0[l]), "w2": rwkv_w2[l].astype(BF16),
                 "a0": row(rwkv_a0[l]), "a2": rwkv_a2[l].astype(BF16), "k_k": row(rwkv_k_k[l]),
                 "k_a": row(rwkv_k_a[l]), "r_k": row(rwkv_r_k[l]), "ln_w": row(rwkv_ln_w[l]),
                 "ln_b": row(rwkv_ln_b[l])},
        "ssm": {"conv_w": ssm_conv_w[l], "conv_b": row(ssm_conv_b[l]),
                "dt_bias": pad_h(ssm_dt_bias[l]), "a_log": pad_h(ssm_A_log[l]),
                "d_x": row(jnp.repeat(ssm_D[l], HEAD_DIM)), "norm_w": row(ssm_norm_w[l])},
        "sb_qw": row(jnp.tile(sb_q_norm_w[l], PAIR)),
        "sb_kw": row(jnp.tile(sb_k_norm_w[l], H_C)),
    }


def _layer(x, prm, consts, B, T, t_valid, s_rwkv_bd, shift, s_ssm, conv_buf, k_past, v_past,
           layer, depth, kv_all, tm_in, tm_kv, tm_out, tq):
    ua, ga, zb, xbc, dt, qc, kc, vc, gc = _in_proj(x, prm["norm_w"], prm["w_in"], tm_in)
    oa, s_rwkv_bd, shift = _rwkv(ua, ga, s_rwkv_bd, shift, prm["rwkv"], consts["tri"], B, T, t_valid)
    ob, s_ssm, conv_buf = _ssd(xbc, zb, dt, s_ssm, conv_buf, prm["ssm"], consts, B, T, t_valid)
    kn, vb, k_new, v_new = _kv_prep(kc, vc, prm["sb_kw"], B, T, tm_kv, layer, depth, kv_all)
    if k_past is None:
        q_start, t_k = 0, T
    else:
        q_start = k_past.shape[1]
        t_k = -(-(q_start + T) // (KG * KB)) * (KG * KB)
        tail = jnp.zeros((B, t_k - q_start - T, D_C), BF16)
        cat = lambda past, new: jnp.concatenate(
            [past, new.reshape(B, T, D_C), tail], axis=1).reshape(B * t_k, D_C)
        kn, vb = cat(k_past, kn), cat(v_past, vb)
    oc = _sb_attn(qc, kn, vb, gc, prm["sb_qw"], consts["mcat"], B, T, t_k, tq, q_start)
    y = _out_proj(x, oa, ob, oc, prm["w_out"], tm_out)
    return y, (s_rwkv_bd, shift, s_ssm, conv_buf, k_new, v_new)


def _packed_cache(c):
    b, h, p, d = c.shape
    return c.transpose(0, 2, 1, 3).reshape(b, p, h * d).astype(BF16)


def kernel(x_prompt, x_sample, state_rwkv, state_rwkv_shift, state_ssm, state_conv, cache_sb_k, cache_sb_v,
           norm_w, w_in, w_out, rwkv_mu, rwkv_w0, rwkv_w2, rwkv_a0, rwkv_a2, rwkv_k_k, rwkv_k_a, rwkv_r_k,
           rwkv_ln_w, rwkv_ln_b, ssm_conv_w, ssm_conv_b, ssm_dt_bias, ssm_A_log, ssm_D, ssm_norm_w,
           sb_q_norm_w, sb_k_norm_w):
    bp, tp, _ = x_prompt.shape
    bs, ts, _ = x_sample.shape
    depth = w_in.shape[0]
    ts_pad = -(-ts // CHUNK) * CHUNK
    consts = _constants()

    yp = x_prompt.reshape(bp * tp, D_MODEL)
    ys = jnp.pad(x_sample, ((0, 0), (0, ts_pad - ts), (0, 0))).reshape(bs * ts_pad, D_MODEL)

    new_p = [[] for _ in range(4)]
    new_s = [[] for _ in range(4)]
    kv_p = kv_s = None
    for l in range(depth):
        prm = _layer_params(l, norm_w, w_in, w_out, rwkv_mu, rwkv_w0, rwkv_w2, rwkv_a0, rwkv_a2,
                            rwkv_k_k, rwkv_k_a, rwkv_r_k, rwkv_ln_w, rwkv_ln_b, ssm_conv_w, ssm_conv_b,
                            ssm_dt_bias, ssm_A_log, ssm_D, ssm_norm_w, sb_q_norm_w, sb_k_norm_w)
        yp, st_p = _layer(
            yp, prm, consts, bp, tp, tp,
            jnp.zeros((bp, N_PAIR_A, LANES, LANES), F32), jnp.zeros((bp, 1, W_SHIFT), F32),
            jnp.zeros((bp, N_PAIR_B, LANES, D_STATE), F32), jnp.zeros((bp, CONV_W - 1, CONV_DIM), F32),
            None, None, l, depth, kv_p, tm_in=256, tm_kv=256, tm_out=512, tq=2 * KB)
        ys, st_s = _layer(
            ys, prm, consts, bs, ts_pad, ts,
            _to_block_diag(state_rwkv[l]), state_rwkv_shift[l],
            state_ssm[l].reshape(bs, N_PAIR_B, LANES, D_STATE), state_conv[l],
            _packed_cache(cache_sb_k[l]), _packed_cache(cache_sb_v[l]),
            l, depth, kv_s, tm_in=256, tm_kv=ts_pad, tm_out=512, tq=ts_pad)
        kv_p, kv_s = st_p[4:], st_s[4:]
        for i in range(4):
            new_p[i].append(st_p[i])
            new_s[i].append(st_s[i])

    def finish(y, st, kv_all, b, t_pad, t):
        s_rwkv, shift, s_ssm, conv = [jnp.stack(v) for v in st]
        k_new, v_new = kv_all
        return (y.reshape(b, t_pad, D_MODEL)[:, :t],
                (_from_block_diag(s_rwkv.reshape(depth * b, N_PAIR_A, LANES, LANES))
                 .reshape(depth, b, H_A, HEAD_DIM, HEAD_DIM)),
                shift,
                s_ssm.reshape(depth, b, H_B, HEAD_DIM, D_STATE),
                conv,
                k_new[:, :, :, :t],
                v_new[:, :, :, :t])

    yp, *rest_p = finish(yp, new_p, kv_p, bp, tp, tp)
    ys, *rest_s = finish(ys, new_s, kv_s, bs, ts_pad, ts)
    return (yp, ys, *rest_p, *rest_s)
```

```python
import functools

import jax
import jax.numpy as jnp
from jax import lax
from jax.experimental import pallas as pl
from jax.experimental.pallas import tpu as pltpu

F32 = jnp.float32
BF16 = jnp.bfloat16

D_MODEL = 1024
D_MIX = 2 * D_MODEL
HEAD_DIM = 64
D_A = 768
H_A = D_A // HEAD_DIM
R_W = 64
R_A = 64
GN_EPS = 64e-5
D_B = 768
H_B = D_B // HEAD_DIM
N_GROUPS = 2
D_STATE = 128
CONV_W = 4
CONV_DIM = D_B + 2 * N_GROUPS * D_STATE
D_C = 512
H_C = D_C // HEAD_DIM
W_SHIFT = 3 * D_A + R_W + R_A
SB_SCALE = HEAD_DIM ** -0.5

LANES = 128
PAIR = LANES // HEAD_DIM
CHUNK = 64
VMEM_LIMIT = 56 * 1024 * 1024

N_PAIR_A = H_A // PAIR
N_PAIR_B = H_B // PAIR
N_PAIR_C = H_C // PAIR
DT_PAD = LANES

IN_WIDTHS = (W_SHIFT, D_A, D_B, CONV_DIM, DT_PAD, D_C, D_C, D_C, D_C)


def _cparams(sem):
    return pltpu.CompilerParams(dimension_semantics=sem, vmem_limit_bytes=VMEM_LIMIT)


def _bdot(a, b):
    return jnp.dot(a.astype(BF16), b.astype(BF16), preferred_element_type=F32)


def _bdot_nt(a, b):
    return lax.dot_general(a.astype(BF16), b.astype(BF16), (((1,), (1,)), ((), ())),
                           preferred_element_type=F32)


def _bdot_tn(a, b):
    return lax.dot_general(a.astype(BF16), b.astype(BF16), (((0,), (0,)), ((), ())),
                           preferred_element_type=F32)


def _split_bf16(x, n):
    parts, r = [], x
    for i in range(n):
        p = r.astype(BF16)
        parts.append(p)
        if i + 1 < n:
            r = r - p.astype(F32)
    return parts


def _sel_dot(sel, x, n=3):
    acc = None
    for p in _split_bf16(x, n):
        d = jnp.dot(sel, p, preferred_element_type=F32)
        acc = d if acc is None else acc + d
    return acc


def _dot_sel(x, sel, n=3):
    acc = None
    for p in _split_bf16(x, n):
        d = jnp.dot(p, sel, preferred_element_type=F32)
        acc = d if acc is None else acc + d
    return acc


def _lane_lo(shape):
    return lax.broadcasted_iota(jnp.int32, shape, len(shape) - 1) < HEAD_DIM


def _head_sum(x, lo):
    s_lo = jnp.sum(jnp.where(lo, x, 0.0), axis=-1, keepdims=True)
    s_hi = jnp.sum(jnp.where(lo, 0.0, x), axis=-1, keepdims=True)
    return jnp.where(lo, s_lo, s_hi)


def _stack_heads(x, lo):
    zero = jnp.zeros_like(x)
    return jnp.concatenate([jnp.where(lo, x, zero), jnp.where(lo, zero, x)], axis=0)


def _silu(x):
    return x * jax.nn.sigmoid(x)


def _softplus(x):
    return jnp.maximum(x, 0.0) + jnp.log1p(jnp.exp(-jnp.abs(x)))


def _softplus_abs(x):
    return jnp.maximum(x, 0.0) + jnp.log(1.0 + jnp.exp(-jnp.abs(x)))


def _in_proj_kernel(x_ref, nw_ref, w_ref, *out_refs):
    x = x_ref[...]
    ms = jnp.mean(x * x, axis=-1, keepdims=True)
    h = (x * lax.rsqrt(ms + 1e-6) * nw_ref[...]).astype(BF16)
    off = 0
    for o_ref, wd in zip(out_refs, IN_WIDTHS):
        o_ref[...] = jnp.dot(h, w_ref[:, off:off + wd], preferred_element_type=F32)
        off += wd


def _in_proj(x, norm_w, w_cat, tm):
    n = x.shape[0]
    n_cols = sum(IN_WIDTHS)
    return pl.pallas_call(
        _in_proj_kernel,
        grid=(n // tm,),
        in_specs=[pl.BlockSpec((tm, D_MODEL), lambda i: (i, 0)),
                  pl.BlockSpec((1, D_MODEL), lambda i: (0, 0)),
                  pl.BlockSpec((D_MODEL, n_cols), lambda i: (0, 0))],
        out_specs=[pl.BlockSpec((tm, wd), lambda i: (i, 0)) for wd in IN_WIDTHS],
        out_shape=[jax.ShapeDtypeStruct((n, wd), F32) for wd in IN_WIDTHS],
        compiler_params=_cparams(("parallel",)),
        name="in_proj",
    )(x, norm_w, w_cat)


def _rwkv_kernel(ua_ref, ga_ref, s0_ref, sh0_ref, mu_ref, w0_ref, w2_ref, a0_ref, a2_ref,
                 kkw_ref, kaw_ref, rkw_ref, lnw_ref, lnb_ref, tri_ref,
                 oa_ref, st_ref, sht_ref,
                 s_scr, prev_scr, us_scr, *, n_t, last_valid):
    C = CHUNK
    t = pl.program_id(1)

    @pl.when(t == 0)
    def _():
        s_scr[...] = s0_ref[0]
        prev_scr[...] = sh0_ref[0]

    u = ua_ref[...]
    row = lax.broadcasted_iota(jnp.int32, (C, 1), 0)
    u_prev = jnp.where(row == 0, prev_scr[...], pltpu.roll(u, 1, axis=0))
    us_scr[...] = u + (u_prev - u) * mu_ref[...]
    prev_scr[...] = u[C - 1:C, :]

    padded = last_valid < C
    valid = jnp.logical_or(t < n_t - 1, row < last_valid) if padded else None

    w_lo = us_scr[:, 3 * D_A:3 * D_A + R_W]
    a_lo = us_scr[:, 3 * D_A + R_W:W_SHIFT]
    wl = w0_ref[...] + _bdot(jnp.tanh(w_lo), w2_ref[...])
    lw = -jnp.exp(-_softplus(-wl) - 0.5)
    a_all = jax.nn.sigmoid(a0_ref[...] + _bdot(a_lo, a2_ref[...]))
    if padded:
        lw = jnp.where(valid, lw, 0.0)
    cum_all = _sel_dot(tri_ref[...], lw)

    lo = _lane_lo((C, LANES))
    ri = lax.broadcasted_iota(jnp.int32, (2 * C, 2 * C), 0) & (C - 1)
    ci = lax.broadcasted_iota(jnp.int32, (2 * C, 2 * C), 1) & (C - 1)
    strict = ci < ri
    incl = ci <= ri

    pairs = range(N_PAIR_A)
    lanes = [slice(p * LANES, (p + 1) * LANES) for p in pairs]
    lhs, rhs, ends, vs, w_all, extra = [], [], [], [], [], []
    for p in pairs:
        sl = lanes[p]
        r = us_scr[:, p * LANES:(p + 1) * LANES]
        k = us_scr[:, D_A + p * LANES:D_A + (p + 1) * LANES]
        v = us_scr[:, 2 * D_A + p * LANES:2 * D_A + (p + 1) * LANES]
        a = a_all[:, sl]
        cum = cum_all[:, sl]
        kk = k * kkw_ref[:, sl]
        kkn = kk / jnp.maximum(jnp.sqrt(_head_sum(kk * kk, lo)), 1e-12)
        k2 = k * (1.0 + (a - 1.0) * kaw_ref[:, sl])
        vv = v
        if padded:
            kkn = jnp.where(valid, kkn, 0.0)
            k2 = jnp.where(valid, k2, 0.0)
            vv = jnp.where(valid, v, 0.0)
        cum_last = cum[C - 1:C, :]
        w_inc = jnp.exp(cum)
        w_exc = jnp.exp(cum - lw[:, sl])
        w_inv = jnp.exp(-cum)
        w_end = jnp.exp(cum_last - cum)
        w_all.append(jnp.exp(cum_last))
        ad = kkn * a
        lhs.append(jnp.concatenate(
            [_stack_heads(-kkn * w_exc, lo), _stack_heads(r * w_inc, lo)], axis=0).astype(BF16))
        rhs.append(jnp.concatenate(
            [_stack_heads(ad * w_inv, lo), _stack_heads(k2 * w_inv, lo)], axis=0).astype(BF16))
        ends.append(jnp.concatenate(
            [_stack_heads(ad * w_end, lo), _stack_heads(k2 * w_end, lo)], axis=0).astype(BF16))
        vs.append(_stack_heads(vv, lo).astype(BF16))
        bonus = _head_sum(r * k2 * rkw_ref[:, sl], lo)
        extra.append((bonus * v, _silu(ga_ref[:, sl])))

    g = [_bdot_nt(lhs[p], rhs[p]) for p in pairs]
    ps = [_bdot_nt(lhs[p], s_scr[p]) for p in pairs]
    x = [ps[p][0:2 * C] + _bdot(jnp.where(strict, g[p][0:2 * C, 2 * C:4 * C], 0.0), vs[p])
         for p in pairs]
    apow = [jnp.where(strict, g[p][0:2 * C, 0:2 * C], 0.0).astype(BF16) for p in pairs]
    n_sq = C.bit_length() - 1
    for i in range(n_sq):
        x = [x[p] + _bdot(apow[p], x[p]) for p in pairs]
        if i + 1 < n_sq:
            apow = [_bdot(apow[p], apow[p]).astype(BF16) for p in pairs]
    for p in pairs:
        sl = lanes[p]
        b_ab = jnp.where(incl, g[p][2 * C:4 * C, 0:2 * C], 0.0).astype(BF16)
        b_ak = jnp.where(incl, g[p][2 * C:4 * C, 2 * C:4 * C], 0.0).astype(BF16)
        uv = jnp.concatenate([x[p].astype(BF16), vs[p]], axis=0)
        ys = ps[p][2 * C:4 * C] + _bdot(jnp.concatenate([b_ab, b_ak], axis=1), uv)
        y = ys[0:C] + ys[C:2 * C]
        s_scr[p] = s_scr[p] * w_all[p] + _bdot_tn(uv, ends[p])

        mean = _head_sum(y, lo) * (1.0 / HEAD_DIM)
        d = y - mean
        var = _head_sum(d * d, lo) * (1.0 / HEAD_DIM)
        yn = d * lax.rsqrt(var + GN_EPS) * lnw_ref[:, sl] + lnb_ref[:, sl]
        bv, gate = extra[p]
        oa_ref[:, sl] = ((yn + bv) * gate).astype(oa_ref.dtype)

    @pl.when(t == n_t - 1)
    def _():
        st_ref[0] = s_scr[...]
        sht_ref[0] = ua_ref[last_valid - 1:last_valid, :]


def _rwkv(ua, ga, s0_bd, sh0, prm, tri, B, T, t_valid):
    n_t = T // CHUNK
    last_valid = t_valid - (n_t - 1) * CHUNK
    assert 1 <= last_valid <= CHUNK
    row = lambda b, t: (b * n_t + t, 0)
    const = lambda b, t: (0, 0)
    vec = lambda w: pl.BlockSpec((1, w), const)
    kern = functools.partial(_rwkv_kernel, n_t=n_t, last_valid=last_valid)
    return pl.pallas_call(
        kern,
        grid=(B, n_t),
        in_specs=[pl.BlockSpec((CHUNK, W_SHIFT), row),
                  pl.BlockSpec((CHUNK, D_A), row),
                  pl.BlockSpec((1, N_PAIR_A, LANES, LANES), lambda b, t: (b, 0, 0, 0)),
                  pl.BlockSpec((1, 1, W_SHIFT), lambda b, t: (b, 0, 0)),
                  vec(W_SHIFT), vec(D_A), pl.BlockSpec((R_W, D_A), const),
                  vec(D_A), pl.BlockSpec((R_A, D_A), const),
                  vec(D_A), vec(D_A), vec(D_A), vec(D_A), vec(D_A),
                  pl.BlockSpec((CHUNK, CHUNK), const)],
        out_specs=[pl.BlockSpec((CHUNK, D_A), row),
                   pl.BlockSpec((1, N_PAIR_A, LANES, LANES), lambda b, t: (b, 0, 0, 0)),
                   pl.BlockSpec((1, 1, W_SHIFT), lambda b, t: (b, 0, 0))],
        out_shape=[jax.ShapeDtypeStruct((B * T, D_A), BF16),
                   jax.ShapeDtypeStruct((B, N_PAIR_A, LANES, LANES), F32),
                   jax.ShapeDtypeStruct((B, 1, W_SHIFT), F32)],
        scratch_shapes=[pltpu.VMEM((N_PAIR_A, LANES, LANES), F32),
                        pltpu.VMEM((1, W_SHIFT), F32),
                        pltpu.VMEM((CHUNK, W_SHIFT), F32)],
        compiler_params=_cparams(("parallel", "arbitrary")),
        name="rwkv7_chunk",
    )(ua, ga, s0_bd, sh0, prm["mu"], prm["w0"], prm["w2"], prm["a0"], prm["a2"],
      prm["k_k"], prm["k_a"], prm["r_k"], prm["ln_w"], prm["ln_b"], tri)


CONV_PAD = 8


def _ssd_kernel(xbc_ref, z_ref, dt_ref, s0_ref, conv0_ref, cw_ref, cb_ref, dtb_ref, alog_ref,
                dx_ref, nw_ref, tri_ref, e64_ref, mrow_ref, mtril_ref,
                ob_ref, st_ref, convt_ref,
                s_scr, ext_scr, *, n_t, last_valid):
    C = CHUNK
    t = pl.program_id(1)
    n_prev = CONV_W - 1

    @pl.when(t == 0)
    def _():
        for p in range(N_PAIR_B):
            s_scr[p] = s0_ref[0, p].T
        ext_scr[CONV_PAD - n_prev:CONV_PAD, :] = conv0_ref[0]

    ext_scr[CONV_PAD:CONV_PAD + C, :] = xbc_ref[...]
    conv = cb_ref[...]
    for i in range(CONV_W):
        conv = conv + ext_scr[CONV_PAD - n_prev + i:CONV_PAD - n_prev + i + C, :] * cw_ref[i:i + 1, :]

    @pl.when(t == n_t - 1)
    def _():
        convt_ref[0] = ext_scr[CONV_PAD + last_valid - n_prev:CONV_PAD + last_valid, :]

    ext_scr[CONV_PAD - n_prev:CONV_PAD, :] = ext_scr[CONV_PAD + C - n_prev:CONV_PAD + C, :]

    xa = _silu(conv)
    xs = xa[:, 0:D_B]
    bm = xa[:, D_B:D_B + N_GROUPS * D_STATE]
    cm = xa[:, D_B + N_GROUPS * D_STATE:CONV_DIM]

    dtv = _softplus(dt_ref[...] + dtb_ref[...])
    if last_valid < C:
        row = lax.broadcasted_iota(jnp.int32, (C, 1), 0)
        dtv = jnp.where(jnp.logical_or(t < n_t - 1, row < last_valid), dtv, 0.0)
    da = dtv * (-jnp.exp(alog_ref[...]))
    a_cs = _sel_dot(tri_ref[...], da)
    ex = _dot_sel(jnp.concatenate([dtv, da, a_cs], axis=0), e64_ref[...])
    dt_x, da_x, acs_x = ex[0:C], ex[C:2 * C], ex[2 * C:3 * C]
    acs_row = jnp.sum(da_x * mrow_ref[...], axis=0, keepdims=True)
    seg = jnp.where(mtril_ref[...] > 0.0, jnp.exp(acs_x - acs_row), 0.0)

    hpg = H_B // N_GROUPS
    cb_tiles = []
    for g in range(N_GROUPS):
        b_g = bm[:, g * D_STATE:(g + 1) * D_STATE]
        c_g = cm[:, g * D_STATE:(g + 1) * D_STATE]
        cb_tiles.append(_bdot_nt(c_g, jnp.concatenate([b_g] * hpg, axis=0)))
    scores = jnp.concatenate(cb_tiles, axis=1) * seg
    xdt = xs * dt_x
    acs_last = acs_x[C - 1:C, :]
    e_in = jnp.exp(acs_x)
    e_end = jnp.exp(acs_last - acs_x)
    e_all = jnp.exp(acs_last)

    lo = _lane_lo((C, LANES))
    pairs_per_group = N_PAIR_B // N_GROUPS
    pairs = range(N_PAIR_B)
    lanes = [slice(p * LANES, (p + 1) * LANES) for p in pairs]
    b_bf = [bm[:, g * D_STATE:(g + 1) * D_STATE].astype(BF16) for g in range(N_GROUPS)]
    c_bf = [cm[:, g * D_STATE:(g + 1) * D_STATE].astype(BF16) for g in range(N_GROUPS)]
    y_diag = [_bdot(scores[:, lanes[p]], _stack_heads(xdt[:, lanes[p]], lo)) for p in pairs]
    y_off = [_bdot(c_bf[p // pairs_per_group], s_scr[p]) for p in pairs]
    s_add = [_bdot_tn(b_bf[p // pairs_per_group], xdt[:, lanes[p]] * e_end[:, lanes[p]]) for p in pairs]
    ys = []
    for p in pairs:
        sl = lanes[p]
        s_scr[p] = s_scr[p] * e_all[:, sl] + s_add[p]
        y = y_diag[p] + y_off[p] * e_in[:, sl] + dx_ref[:, sl] * xs[:, sl]
        ys.append(y * _silu(z_ref[:, sl]))

    gw = D_B // N_GROUPS
    for g in range(N_GROUPS):
        yg = jnp.concatenate(ys[g * pairs_per_group:(g + 1) * pairs_per_group], axis=1)
        ms = jnp.mean(yg * yg, axis=-1, keepdims=True)
        ob_ref[:, g * gw:(g + 1) * gw] = (
            yg * lax.rsqrt(ms + 1e-5) * nw_ref[:, g * gw:(g + 1) * gw]).astype(ob_ref.dtype)

    @pl.when(t == n_t - 1)
    def _():
        for p in range(N_PAIR_B):
            st_ref[0, p] = s_scr[p].T


def _ssd(xbc, zb, dt, s0, conv0, prm, consts, B, T, t_valid):
    n_t = T // CHUNK
    last_valid = t_valid - (n_t - 1) * CHUNK
    assert CONV_W - 1 <= last_valid <= CHUNK
    row = lambda b, t: (b * n_t + t, 0)
    const = lambda b, t: (0, 0)
    vec = lambda w: pl.BlockSpec((1, w), const)
    kern = functools.partial(_ssd_kernel, n_t=n_t, last_valid=last_valid)
    return pl.pallas_call(
        kern,
        grid=(B, n_t),
        in_specs=[pl.BlockSpec((CHUNK, CONV_DIM), row),
                  pl.BlockSpec((CHUNK, D_B), row),
                  pl.BlockSpec((CHUNK, DT_PAD), row),
                  pl.BlockSpec((1, N_PAIR_B, LANES, D_STATE), lambda b, t: (b, 0, 0, 0)),
                  pl.BlockSpec((1, CONV_W - 1, CONV_DIM), lambda b, t: (b, 0, 0)),
                  pl.BlockSpec((CONV_W, CONV_DIM), const), vec(CONV_DIM),
                  vec(DT_PAD), vec(DT_PAD), vec(D_B), vec(D_B),
                  pl.BlockSpec((CHUNK, CHUNK), const),
                  pl.BlockSpec((DT_PAD, D_B), const),
                  pl.BlockSpec((CHUNK, D_B), const),
                  pl.BlockSpec((CHUNK, D_B), const)],
        out_specs=[pl.BlockSpec((CHUNK, D_B), row),
                   pl.BlockSpec((1, N_PAIR_B, LANES, D_STATE), lambda b, t: (b, 0, 0, 0)),
                   pl.BlockSpec((1, CONV_W - 1, CONV_DIM), lambda b, t: (b, 0, 0))],
        out_shape=[jax.ShapeDtypeStruct((B * T, D_B), BF16),
                   jax.ShapeDtypeStruct((B, N_PAIR_B, LANES, D_STATE), F32),
                   jax.ShapeDtypeStruct((B, CONV_W - 1, CONV_DIM), F32)],
        scratch_shapes=[pltpu.VMEM((N_PAIR_B, D_STATE, LANES), F32),
                        pltpu.VMEM((CONV_PAD + CHUNK, CONV_DIM), F32)],
        compiler_params=_cparams(("parallel", "arbitrary")),
        name="ssd_chunk",
    )(xbc, zb, dt, s0, conv0, prm["conv_w"], prm["conv_b"], prm["dt_bias"], prm["a_log"],
      prm["d_x"], prm["norm_w"], consts["tri"], consts["e64"], consts["mrow"], consts["mtril"])


def _kv_prep_kernel(k_ref, v_ref, w_ref, *refs):
    kn_ref, vb_ref, knew_ref, vnew_ref = refs[-4:]
    tm = k_ref.shape[0]
    lo = _lane_lo((tm, LANES))
    for p in range(N_PAIR_C):
        sl = slice(p * LANES, (p + 1) * LANES)
        k = k_ref[:, sl]
        ms = _head_sum(k * k, lo) * (1.0 / HEAD_DIM)
        kn = k * lax.rsqrt(ms + 1e-6) * w_ref[:, sl]
        v = v_ref[:, sl]
        kn_ref[:, sl] = kn.astype(kn_ref.dtype)
        vb_ref[:, sl] = v.astype(vb_ref.dtype)
        for j in range(PAIR):
            h = p * PAIR + j
            knew_ref[0, 0, h] = kn[:, j * HEAD_DIM:(j + 1) * HEAD_DIM]
            vnew_ref[0, 0, h] = v[:, j * HEAD_DIM:(j + 1) * HEAD_DIM]


def _kv_prep(k, v, knw, B, T, tm, layer, depth, kv_all):
    n_t = T // tm
    row = lambda b, t: (b * n_t + t, 0)
    hm = pl.BlockSpec((1, 1, H_C, tm, HEAD_DIM), lambda b, t: (layer, b, 0, t, 0))
    hm_shape = jax.ShapeDtypeStruct((depth, B, H_C, T, HEAD_DIM), F32)
    in_specs = [pl.BlockSpec((tm, D_C), row), pl.BlockSpec((tm, D_C), row),
                pl.BlockSpec((1, D_C), lambda b, t: (0, 0))]
    args, aliases = [k, v, knw], {}
    if kv_all is not None:
        in_specs += [pl.BlockSpec(memory_space=pl.ANY)] * 2
        args += list(kv_all)
        aliases = {3: 2, 4: 3}
    return pl.pallas_call(
        _kv_prep_kernel,
        grid=(B, n_t),
        in_specs=in_specs,
        out_specs=[pl.BlockSpec((tm, D_C), row), pl.BlockSpec((tm, D_C), row), hm, hm],
        out_shape=[jax.ShapeDtypeStruct((B * T, D_C), BF16),
                   jax.ShapeDtypeStruct((B * T, D_C), BF16), hm_shape, hm_shape],
        input_output_aliases=aliases,
        compiler_params=_cparams(("parallel", "parallel")),
        name="sb_kv_prep",
    )(*args)


KB = LANES


KG = 4
MASKED_LOGIT = -1e30


def _sb_attn_kernel(q_ref, k_ref, v_ref, g_ref, qw_ref, mcat_ref, o_ref, *, tq, q_start, n_q):
    qi = pl.program_id(2)
    gk = KG * KB
    lo = _lane_lo((tq, LANES))
    q = q_ref[...]
    ms = _head_sum(q * q, lo) * (1.0 / HEAD_DIM)
    qn = q * lax.rsqrt(ms + 1e-6) * qw_ref[...] * SB_SCALE
    zero = jnp.zeros_like(qn)
    q_both = jnp.concatenate([jnp.where(lo, qn, zero), jnp.where(lo, zero, qn)], axis=0).astype(BF16)
    q0 = q_start + qi * tq
    g_top = q0 // gk
    bp = 2 * KB
    n_bp_full = KG // 2

    def group(gi, carry, masked, n_bp):
        c_heads, acc = list(carry[:PAIR]), carry[PAIR]
        nk = n_bp * bp
        ks = pl.multiple_of(gi * gk, gk)
        kg = k_ref[pl.ds(ks, nk), :]
        vg = v_ref[pl.ds(ks, nk), :]
        lo_kv = _lane_lo((nk, LANES))
        z = _bdot_nt(q_both, kg)
        if masked:
            k_pos = ks + lax.broadcasted_iota(jnp.int32, (PAIR * tq, nk), 1)
            q_pos = q0 + (lax.broadcasted_iota(jnp.int32, (PAIR * tq, nk), 0) & (tq - 1))
            z = jnp.where(k_pos < q_pos, z, MASKED_LOGIT)
        zb = z.astype(BF16)
        l = jnp.log(1.0 + jnp.exp(-jnp.abs(zb)))
        sp = jnp.maximum(zb, 0.0) + l
        d = jnp.minimum(zb, 0.0) - l
        tiles = [sp[h * tq:(h + 1) * tq, j * bp:(j + 1) * bp] for h in range(PAIR) for j in range(n_bp)]
        tc = jnp.dot(jnp.concatenate(tiles, axis=0), mcat_ref[...], preferred_element_type=F32)
        atts = []
        for h in range(PAIR):
            c = c_heads[h]
            att_h = [None] * n_bp
            for j in reversed(range(n_bp)):
                t0 = (h * n_bp + j) * tq
                rows = slice(h * tq, (h + 1) * tq)
                tail = tc[t0:t0 + tq, :]
                ex = (tail + c).astype(BF16) + d[rows, j * bp:(j + 1) * bp]
                att_h[j] = jnp.exp(ex)
                c = c + (tail[:, 0:1] - sp[rows, j * bp:j * bp + 1].astype(F32))
            c_heads[h] = c
            atts.extend(att_h)
        vz = jnp.zeros_like(vg)
        v_st = jnp.concatenate([jnp.where(lo_kv, vg, vz), jnp.where(lo_kv, vz, vg)], axis=0)
        acc = acc + jnp.dot(jnp.concatenate(atts, axis=1), v_st, preferred_element_type=F32)
        return (*c_heads, acc)

    init = (jnp.zeros((tq, 1), F32),) * PAIR + (jnp.zeros((tq, LANES), F32),)
    if n_q == 1:
        carry = group(g_top, init, True, -(-(q_start % gk + tq) // bp))
    else:
        assert tq == bp and n_bp_full == 2
        carry = lax.cond(q0 % gk == 0, lambda: group(g_top, init, True, 1),
                         lambda: group(g_top, init, True, 2))
    carry = lax.fori_loop(0, g_top, lambda i, c: group(g_top - 1 - i, c, False, n_bp_full), carry)
    o_ref[...] = (carry[PAIR] * _silu(g_ref[...])).astype(o_ref.dtype)


def _sb_attn(q, kn, vb, gc, qw, mcat, B, Tq, Tk, tq, q_start):
    nq = Tq // tq
    gk = KG * KB
    assert gk % tq == 0 and q_start % gk == 0 and tq & (tq - 1) == 0
    assert Tk % (2 * KB) == 0 and Tk >= q_start + Tq
    qrow = lambda b, p, i: (b * nq + i, p)
    kv = lambda b, p, i: (b, p)
    kern = functools.partial(_sb_attn_kernel, tq=tq, q_start=q_start, n_q=nq)
    return pl.pallas_call(
        kern,
        grid=(B, N_PAIR_C, nq),
        in_specs=[pl.BlockSpec((tq, LANES), qrow),
                  pl.BlockSpec((Tk, LANES), kv),
                  pl.BlockSpec((Tk, LANES), kv),
                  pl.BlockSpec((tq, LANES), qrow),
                  pl.BlockSpec((1, LANES), lambda b, p, i: (0, 0)),
                  pl.BlockSpec((2 * KB, 2 * KB), lambda b, p, i: (0, 0))],
        out_specs=pl.BlockSpec((tq, LANES), qrow),
        out_shape=jax.ShapeDtypeStruct((B * Tq, D_C), BF16),
        compiler_params=_cparams(("parallel", "parallel", "arbitrary")),
        name="sb_attn",
    )(q, kn, vb, gc, qw, mcat)


def _out_proj_kernel(x_ref, oa_ref, ob_ref, oc_ref, w_ref, y_ref):
    acc = jnp.dot(oa_ref[...], w_ref[0:D_A, :], preferred_element_type=F32)
    acc = acc + jnp.dot(ob_ref[...], w_ref[D_A:D_A + D_B, :], preferred_element_type=F32)
    acc = acc + jnp.dot(oc_ref[...], w_ref[D_A + D_B:D_MIX, :], preferred_element_type=F32)
    y_ref[...] = x_ref[...] + acc


def _out_proj(x, oa, ob, oc, w_out, tm):
    n = x.shape[0]
    blk = lambda w: pl.BlockSpec((tm, w), lambda i: (i, 0))
    return pl.pallas_call(
        _out_proj_kernel,
        grid=(n // tm,),
        in_specs=[blk(D_MODEL), blk(D_A), blk(D_B), blk(D_C),
                  pl.BlockSpec((D_MIX, D_MODEL), lambda i: (0, 0))],
        out_specs=blk(D_MODEL),
        out_shape=jax.ShapeDtypeStruct((n, D_MODEL), F32),
        compiler_params=_cparams(("parallel",)),
        name="out_proj",
    )(x, oa, ob, oc, w_out)


def _constants():
    i64 = jnp.arange(CHUNK)
    lane = jnp.arange(D_B)
    kj = jnp.arange(KB)
    return {
        "tri": (i64[:, None] >= i64[None, :]).astype(BF16),
        "e64": (jnp.arange(DT_PAD)[:, None] == lane[None, :] // HEAD_DIM).astype(BF16),
        "mrow": (i64[:, None] <= lane[None, :] % CHUNK).astype(F32),
        "mtril": (lane[None, :] % CHUNK <= i64[:, None]).astype(F32),
        "mcat": -(jnp.arange(2 * KB)[:, None] > jnp.arange(2 * KB)[None, :]).astype(BF16),
    }


def _pack_w_in(w):
    o1 = W_SHIFT
    o2 = o1 + D_A
    o3 = o2 + D_B
    o4 = o3 + CONV_DIM
    o5 = o4 + H_B
    dt_cols = jnp.pad(w[:, o4:o5], ((0, 0), (0, DT_PAD - H_B)))
    return jnp.concatenate([w[:, :o4], dt_cols, w[:, o5:]], axis=1).astype(BF16)


def _to_block_diag(s):
    b = s.shape[0]
    s = s.reshape(b, N_PAIR_A, PAIR, HEAD_DIM, HEAD_DIM)
    eye = jnp.eye(PAIR, dtype=s.dtype)
    out = s[:, :, :, :, None, :] * eye[None, None, :, None, :, None]
    return out.reshape(b, N_PAIR_A, LANES, LANES)


def _from_block_diag(sbd):
    b = sbd.shape[0]
    s = sbd.reshape(b, N_PAIR_A, PAIR, HEAD_DIM, PAIR, HEAD_DIM)
    s = jnp.stack([s[:, :, j, :, j, :] for j in range(PAIR)], axis=2)
    return s.reshape(b, H_A, HEAD_DIM, HEAD_DIM)


def _layer_params(l, norm_w, w_in, w_out, rwkv_mu, rwkv_w0, rwkv_w2, rwkv_a0, rwkv_a2, rwkv_k_k,
                  rwkv_k_a, rwkv_r_k, rwkv_ln_w, rwkv_ln_b, ssm_conv_w, ssm_conv_b, ssm_dt_bias,
                  ssm_A_log, ssm_D, ssm_norm_w, sb_q_norm_w, sb_k_norm_w):
    row = lambda x: x.reshape(1, -1)
    pad_h = lambda x: jnp.pad(x, (0, DT_PAD - H_B)).reshape(1, DT_PAD)
    return {
        "norm_w": row(norm_w[l]), "w_in": _pack_w_in(w_in[l]), "w_out": w_out[l].astype(BF16),
        "rwkv": {"mu": row(rwkv_mu[l]), "w0": row(rwkv_w0[l]), "w2": rwkv_w2[l].astype(BF16),
                 "a0": row(rwkv_a0[l]), "a2": rwkv_a2[l].astype(BF16), "k_k": row(rwkv_k_k[l]),
                 "k_a": row(rwkv_k_a[l]), "r_k": row(rwkv_r_k[l]), "ln_w": row(rwkv_ln_w[l]),
                 "ln_b": row(rwkv_ln_b[l])},
        "ssm": {"conv_w": ssm_conv_w[l], "conv_b": row(ssm_conv_b[l]),
                "dt_bias": pad_h(ssm_dt_bias[l]), "a_log": pad_h(ssm_A_log[l]),
                "d_x": row(jnp.repeat(ssm_D[l], HEAD_DIM)), "norm_w": row(ssm_norm_w[l])},
        "sb_qw": row(jnp.tile(sb_q_norm_w[l], PAIR)),
        "sb_kw": row(jnp.tile(sb_k_norm_w[l], H_C)),
    }


def _layer(x, prm, consts, B, T, t_valid, s_rwkv_bd, shift, s_ssm, conv_buf, k_past, v_past,
           layer, depth, kv_all, tm_in, tm_kv, tm_out, tq):
    ua, ga, zb, xbc, dt, qc, kc, vc, gc = _in_proj(x, prm["norm_w"], prm["w_in"], tm_in)
    oa, s_rwkv_bd, shift = _rwkv(ua, ga, s_rwkv_bd, shift, prm["rwkv"], consts["tri"], B, T, t_valid)
    ob, s_ssm, conv_buf = _ssd(xbc, zb, dt, s_ssm, conv_buf, prm["ssm"], consts, B, T, t_valid)
    kn, vb, k_new, v_new = _kv_prep(kc, vc, prm["sb_kw"], B, T, tm_kv, layer, depth, kv_all)
    if k_past is None:
        q_start, t_k = 0, T
    else:
        q_start = k_past.shape[1]
        t_k = -(-(q_start + T) // (2 * KB)) * (2 * KB)
        tail = jnp.zeros((B, t_k - q_start - T, D_C), BF16)
        cat = lambda past, new: jnp.concatenate(
            [past, new.reshape(B, T, D_C), tail], axis=1).reshape(B * t_k, D_C)
        kn, vb = cat(k_past, kn), cat(v_past, vb)
    oc = _sb_attn(qc, kn, vb, gc, prm["sb_qw"], consts["mcat"], B, T, t_k, tq, q_start)
    y = _out_proj(x, oa, ob, oc, prm["w_out"], tm_out)
    return y, (s_rwkv_bd, shift, s_ssm, conv_buf, k_new, v_new)


def _cache_pack_kernel(c_ref, o_ref):
    for h in range(H_C):
        o_ref[0, :, h * HEAD_DIM:(h + 1) * HEAD_DIM] = c_ref[0, h].astype(o_ref.dtype)


def _packed_cache(c):
    b, h, p, d = c.shape
    tp = min(p, 512)
    assert p % tp == 0 and (h, d) == (H_C, HEAD_DIM)
    return pl.pallas_call(
        _cache_pack_kernel,
        grid=(b, p // tp),
        in_specs=[pl.BlockSpec((1, h, tp, d), lambda i, j: (i, 0, j, 0))],
        out_specs=pl.BlockSpec((1, tp, h * d), lambda i, j: (i, j, 0)),
        out_shape=jax.ShapeDtypeStruct((b, p, h * d), BF16),
        compiler_params=_cparams(("parallel", "parallel")),
        name="sb_cache_pack",
    )(c)


def kernel(x_prompt, x_sample, state_rwkv, state_rwkv_shift, state_ssm, state_conv, cache_sb_k, cache_sb_v,
           norm_w, w_in, w_out, rwkv_mu, rwkv_w0, rwkv_w2, rwkv_a0, rwkv_a2, rwkv_k_k, rwkv_k_a, rwkv_r_k,
           rwkv_ln_w, rwkv_ln_b, ssm_conv_w, ssm_conv_b, ssm_dt_bias, ssm_A_log, ssm_D, ssm_norm_w,
           sb_q_norm_w, sb_k_norm_w):
    bp, tp, _ = x_prompt.shape
    bs, ts, _ = x_sample.shape
    depth = w_in.shape[0]
    ts_pad = -(-ts // CHUNK) * CHUNK
    consts = _constants()

    yp = x_prompt.reshape(bp * tp, D_MODEL)
    ys = jnp.pad(x_sample, ((0, 0), (0, ts_pad - ts), (0, 0))).reshape(bs * ts_pad, D_MODEL)

    new_p = [[] for _ in range(4)]
    new_s = [[] for _ in range(4)]
    kv_p = kv_s = None
    for l in range(depth):
        prm = _layer_params(l, norm_w, w_in, w_out, rwkv_mu, rwkv_w0, rwkv_w2, rwkv_a0, rwkv_a2,
                            rwkv_k_k, rwkv_k_a, rwkv_r_k, rwkv_ln_w, rwkv_ln_b, ssm_conv_w, ssm_conv_b,
                            ssm_dt_bias, ssm_A_log, ssm_D, ssm_norm_w, sb_q_norm_w, sb_k_norm_w)
        yp, st_p = _layer(
            yp, prm, consts, bp, tp, tp,
            jnp.zeros((bp, N_PAIR_A, LANES, LANES), F32), jnp.zeros((bp, 1, W_SHIFT), F32),
            jnp.zeros((bp, N_PAIR_B, LANES, D_STATE), F32), jnp.zeros((bp, CONV_W - 1, CONV_DIM), F32),
            None, None, l, depth, kv_p, tm_in=256, tm_kv=256, tm_out=512, tq=2 * KB)
        ys, st_s = _layer(
            ys, prm, consts, bs, ts_pad, ts,
            _to_block_diag(state_rwkv[l]), state_rwkv_shift[l],
            state_ssm[l].reshape(bs, N_PAIR_B, LANES, D_STATE), state_conv[l],
            _packed_cache(cache_sb_k[l]), _packed_cache(cache_sb_v[l]),
            l, depth, kv_s, tm_in=256, tm_kv=ts_pad, tm_out=512, tq=ts_pad)
        kv_p, kv_s = st_p[4:], st_s[4:]
        for i in range(4):
            new_p[i].append(st_p[i])
            new_s[i].append(st_s[i])

    def finish(y, st, kv_all, b, t_pad, t):
        s_rwkv, shift, s_ssm, conv = [jnp.stack(v) for v in st]
        k_new, v_new = kv_all
        return (y.reshape(b, t_pad, D_MODEL)[:, :t],
                (_from_block_diag(s_rwkv.reshape(depth * b, N_PAIR_A, LANES, LANES))
                 .reshape(depth, b, H_A, HEAD_DIM, HEAD_DIM)),
                shift,
                s_ssm.reshape(depth, b, H_B, HEAD_DIM, D_STATE),
                conv,
                k_new[:, :, :, :t],
                v_new[:, :, :, :t])

    yp, *rest_p = finish(yp, new_p, kv_p, bp, tp, tp)
    ys, *rest_s = finish(ys, new_s, kv_s, bs, ts_pad, ts)
    return (yp, ys, *rest_p, *rest_s)
```

```python
import functools

import jax
import jax.numpy as jnp
from jax import lax
from jax.experimental import pallas as pl
from jax.experimental.pallas import tpu as pltpu

F32 = jnp.float32
BF16 = jnp.bfloat16

D_MODEL = 1024
D_MIX = 2 * D_MODEL
HEAD_DIM = 64
D_A = 768
H_A = D_A // HEAD_DIM
R_W = 64
R_A = 64
GN_EPS = 64e-5
D_B = 768
H_B = D_B // HEAD_DIM
N_GROUPS = 2
D_STATE = 128
CONV_W = 4
CONV_DIM = D_B + 2 * N_GROUPS * D_STATE
D_C = 512
H_C = D_C // HEAD_DIM
W_SHIFT = 3 * D_A + R_W + R_A
SB_SCALE = HEAD_DIM ** -0.5

LANES = 128
PAIR = LANES // HEAD_DIM
CHUNK = 64
VMEM_LIMIT = 56 * 1024 * 1024

N_PAIR_A = H_A // PAIR
N_PAIR_B = H_B // PAIR
N_PAIR_C = H_C // PAIR
DT_PAD = LANES

IN_WIDTHS = (W_SHIFT, D_A, D_B, CONV_DIM, DT_PAD, D_C, D_C, D_C, D_C)


def _cparams(sem):
    return pltpu.CompilerParams(dimension_semantics=sem, vmem_limit_bytes=VMEM_LIMIT)


def _bdot(a, b):
    return jnp.dot(a.astype(BF16), b.astype(BF16), preferred_element_type=F32)


def _bdot_nt(a, b):
    return lax.dot_general(a.astype(BF16), b.astype(BF16), (((1,), (1,)), ((), ())),
                           preferred_element_type=F32)


def _bdot_tn(a, b):
    return lax.dot_general(a.astype(BF16), b.astype(BF16), (((0,), (0,)), ((), ())),
                           preferred_element_type=F32)


def _split_bf16(x, n):
    parts, r = [], x
    for i in range(n):
        p = r.astype(BF16)
        parts.append(p)
        if i + 1 < n:
            r = r - p.astype(F32)
    return parts


def _sel_dot(sel, x, n=3):
    acc = None
    for p in _split_bf16(x, n):
        d = jnp.dot(sel, p, preferred_element_type=F32)
        acc = d if acc is None else acc + d
    return acc


def _dot_sel(x, sel, n=3):
    acc = None
    for p in _split_bf16(x, n):
        d = jnp.dot(p, sel, preferred_element_type=F32)
        acc = d if acc is None else acc + d
    return acc


def _lane_lo(shape):
    return lax.broadcasted_iota(jnp.int32, shape, len(shape) - 1) < HEAD_DIM


def _head_sum(x, lo):
    s_lo = jnp.sum(jnp.where(lo, x, 0.0), axis=-1, keepdims=True)
    s_hi = jnp.sum(jnp.where(lo, 0.0, x), axis=-1, keepdims=True)
    return jnp.where(lo, s_lo, s_hi)


def _stack_heads(x, lo):
    zero = jnp.zeros_like(x)
    return jnp.concatenate([jnp.where(lo, x, zero), jnp.where(lo, zero, x)], axis=0)


def _silu(x):
    return x * jax.nn.sigmoid(x)


def _softplus(x):
    return jnp.maximum(x, 0.0) + jnp.log1p(jnp.exp(-jnp.abs(x)))


def _softplus_abs(x):
    return jnp.maximum(x, 0.0) + jnp.log(1.0 + jnp.exp(-jnp.abs(x)))


def _in_proj_kernel(x_ref, nw_ref, w_ref, *out_refs):
    x = x_ref[...]
    ms = jnp.mean(x * x, axis=-1, keepdims=True)
    h = (x * lax.rsqrt(ms + 1e-6) * nw_ref[...]).astype(BF16)
    off = 0
    for o_ref, wd in zip(out_refs, IN_WIDTHS):
        o_ref[...] = jnp.dot(h, w_ref[:, off:off + wd], preferred_element_type=F32)
        off += wd


def _in_proj(x, norm_w, w_cat, tm):
    n = x.shape[0]
    n_cols = sum(IN_WIDTHS)
    return pl.pallas_call(
        _in_proj_kernel,
        grid=(n // tm,),
        in_specs=[pl.BlockSpec((tm, D_MODEL), lambda i: (i, 0)),
                  pl.BlockSpec((1, D_MODEL), lambda i: (0, 0)),
                  pl.BlockSpec((D_MODEL, n_cols), lambda i: (0, 0))],
        out_specs=[pl.BlockSpec((tm, wd), lambda i: (i, 0)) for wd in IN_WIDTHS],
        out_shape=[jax.ShapeDtypeStruct((n, wd), F32) for wd in IN_WIDTHS],
        compiler_params=_cparams(("parallel",)),
        name="in_proj",
    )(x, norm_w, w_cat)


def _rwkv_kernel(ua_ref, ga_ref, s0_ref, sh0_ref, mu_ref, w0_ref, w2_ref, a0_ref, a2_ref,
                 kkw_ref, kaw_ref, rkw_ref, lnw_ref, lnb_ref, tri_ref,
                 oa_ref, st_ref, sht_ref,
                 s_scr, prev_scr, us_scr, *, n_t, last_valid):
    C = CHUNK
    t = pl.program_id(1)

    @pl.when(t == 0)
    def _():
        s_scr[...] = s0_ref[0]
        prev_scr[...] = sh0_ref[0]

    u = ua_ref[...]
    row = lax.broadcasted_iota(jnp.int32, (C, 1), 0)
    u_prev = jnp.where(row == 0, prev_scr[...], pltpu.roll(u, 1, axis=0))
    us_scr[...] = u + (u_prev - u) * mu_ref[...]
    prev_scr[...] = u[C - 1:C, :]

    padded = last_valid < C
    valid = jnp.logical_or(t < n_t - 1, row < last_valid) if padded else None

    w_lo = us_scr[:, 3 * D_A:3 * D_A + R_W]
    a_lo = us_scr[:, 3 * D_A + R_W:W_SHIFT]
    wl = w0_ref[...] + _bdot(jnp.tanh(w_lo), w2_ref[...])
    lw = -jnp.exp(-_softplus(-wl) - 0.5)
    a_all = jax.nn.sigmoid(a0_ref[...] + _bdot(a_lo, a2_ref[...]))
    if padded:
        lw = jnp.where(valid, lw, 0.0)
    cum_all = _sel_dot(tri_ref[...], lw)

    lo = _lane_lo((C, LANES))
    ri = lax.broadcasted_iota(jnp.int32, (2 * C, 2 * C), 0) & (C - 1)
    ci = lax.broadcasted_iota(jnp.int32, (2 * C, 2 * C), 1) & (C - 1)
    strict = ci < ri
    incl = ci <= ri

    pairs = range(N_PAIR_A)
    lanes = [slice(p * LANES, (p + 1) * LANES) for p in pairs]
    lhs, rhs, ends, vs, w_all, extra = [], [], [], [], [], []
    for p in pairs:
        sl = lanes[p]
        r = us_scr[:, p * LANES:(p + 1) * LANES]
        k = us_scr[:, D_A + p * LANES:D_A + (p + 1) * LANES]
        v = us_scr[:, 2 * D_A + p * LANES:2 * D_A + (p + 1) * LANES]
        a = a_all[:, sl]
        cum = cum_all[:, sl]
        kk = k * kkw_ref[:, sl]
        kkn = kk / jnp.maximum(jnp.sqrt(_head_sum(kk * kk, lo)), 1e-12)
        k2 = k * (1.0 + (a - 1.0) * kaw_ref[:, sl])
        vv = v
        if padded:
            kkn = jnp.where(valid, kkn, 0.0)
            k2 = jnp.where(valid, k2, 0.0)
            vv = jnp.where(valid, v, 0.0)
        cum_last = cum[C - 1:C, :]
        w_inc = jnp.exp(cum)
        w_exc = jnp.exp(cum - lw[:, sl])
        w_inv = jnp.exp(-cum)
        w_end = jnp.exp(cum_last - cum)
        w_all.append(jnp.exp(cum_last))
        ad = kkn * a
        lhs.append(jnp.concatenate(
            [_stack_heads(-kkn * w_exc, lo), _stack_heads(r * w_inc, lo)], axis=0).astype(BF16))
        rhs.append(jnp.concatenate(
            [_stack_heads(ad * w_inv, lo), _stack_heads(k2 * w_inv, lo)], axis=0).astype(BF16))
        ends.append(jnp.concatenate(
            [_stack_heads(ad * w_end, lo), _stack_heads(k2 * w_end, lo)], axis=0).astype(BF16))
        vs.append(_stack_heads(vv, lo).astype(BF16))
        bonus = _head_sum(r * k2 * rkw_ref[:, sl], lo)
        extra.append((bonus * v, _silu(ga_ref[:, sl])))

    g = [_bdot_nt(lhs[p], rhs[p]) for p in pairs]
    ps = [_bdot_nt(lhs[p], s_scr[p]) for p in pairs]
    x = [ps[p][0:2 * C] + _bdot(jnp.where(strict, g[p][0:2 * C, 2 * C:4 * C], 0.0), vs[p])
         for p in pairs]
    apow = [jnp.where(strict, g[p][0:2 * C, 0:2 * C], 0.0).astype(BF16) for p in pairs]
    n_sq = C.bit_length() - 1
    for i in range(n_sq):
        x = [x[p] + _bdot(apow[p], x[p]) for p in pairs]
        if i + 1 < n_sq:
            apow = [_bdot(apow[p], apow[p]).astype(BF16) for p in pairs]
    for p in pairs:
        sl = lanes[p]
        b_ab = jnp.where(incl, g[p][2 * C:4 * C, 0:2 * C], 0.0).astype(BF16)
        b_ak = jnp.where(incl, g[p][2 * C:4 * C, 2 * C:4 * C], 0.0).astype(BF16)
        uv = jnp.concatenate([x[p].astype(BF16), vs[p]], axis=0)
        ys = ps[p][2 * C:4 * C] + _bdot(jnp.concatenate([b_ab, b_ak], axis=1), uv)
        y = ys[0:C] + ys[C:2 * C]
        s_scr[p] = s_scr[p] * w_all[p] + _bdot_tn(uv, ends[p])

        mean = _head_sum(y, lo) * (1.0 / HEAD_DIM)
        d = y - mean
        var = _head_sum(d * d, lo) * (1.0 / HEAD_DIM)
        yn = d * lax.rsqrt(var + GN_EPS) * lnw_ref[:, sl] + lnb_ref[:, sl]
        bv, gate = extra[p]
        oa_ref[:, sl] = ((yn + bv) * gate).astype(oa_ref.dtype)

    @pl.when(t == n_t - 1)
    def _():
        st_ref[0] = s_scr[...]
        sht_ref[0] = ua_ref[last_valid - 1:last_valid, :]


def _rwkv(ua, ga, s0_bd, sh0, prm, tri, B, T, t_valid):
    n_t = T // CHUNK
    last_valid = t_valid - (n_t - 1) * CHUNK
    assert 1 <= last_valid <= CHUNK
    row = lambda b, t: (b * n_t + t, 0)
    const = lambda b, t: (0, 0)
    vec = lambda w: pl.BlockSpec((1, w), const)
    kern = functools.partial(_rwkv_kernel, n_t=n_t, last_valid=last_valid)
    return pl.pallas_call(
        kern,
        grid=(B, n_t),
        in_specs=[pl.BlockSpec((CHUNK, W_SHIFT), row),
                  pl.BlockSpec((CHUNK, D_A), row),
                  pl.BlockSpec((1, N_PAIR_A, LANES, LANES), lambda b, t: (b, 0, 0, 0)),
                  pl.BlockSpec((1, 1, W_SHIFT), lambda b, t: (b, 0, 0)),
                  vec(W_SHIFT), vec(D_A), pl.BlockSpec((R_W, D_A), const),
                  vec(D_A), pl.BlockSpec((R_A, D_A), const),
                  vec(D_A), vec(D_A), vec(D_A), vec(D_A), vec(D_A),
                  pl.BlockSpec((CHUNK, CHUNK), const)],
        out_specs=[pl.BlockSpec((CHUNK, D_A), row),
                   pl.BlockSpec((1, N_PAIR_A, LANES, LANES), lambda b, t: (b, 0, 0, 0)),
                   pl.BlockSpec((1, 1, W_SHIFT), lambda b, t: (b, 0, 0))],
        out_shape=[jax.ShapeDtypeStruct((B * T, D_A), BF16),
                   jax.ShapeDtypeStruct((B, N_PAIR_A, LANES, LANES), F32),
                   jax.ShapeDtypeStruct((B, 1, W_SHIFT), F32)],
        scratch_shapes=[pltpu.VMEM((N_PAIR_A, LANES, LANES), F32),
                        pltpu.VMEM((1, W_SHIFT), F32),
                        pltpu.VMEM((CHUNK, W_SHIFT), F32)],
        compiler_params=_cparams(("parallel", "arbitrary")),
        name="rwkv7_chunk",
    )(ua, ga, s0_bd, sh0, prm["mu"], prm["w0"], prm["w2"], prm["a0"], prm["a2"],
      prm["k_k"], prm["k_a"], prm["r_k"], prm["ln_w"], prm["ln_b"], tri)


CONV_PAD = 8


def _ssd_kernel(xbc_ref, z_ref, dt_ref, s0_ref, conv0_ref, cw_ref, cb_ref, dtb_ref, alog_ref,
                dx_ref, nw_ref, tri_ref, e64_ref, mrow_ref, mtril_ref,
                ob_ref, st_ref, convt_ref,
                s_scr, ext_scr, *, n_t, last_valid):
    C = CHUNK
    t = pl.program_id(1)
    n_prev = CONV_W - 1

    @pl.when(t == 0)
    def _():
        for p in range(N_PAIR_B):
            s_scr[p] = s0_ref[0, p].T
        ext_scr[CONV_PAD - n_prev:CONV_PAD, :] = conv0_ref[0]

    ext_scr[CONV_PAD:CONV_PAD + C, :] = xbc_ref[...]
    conv = cb_ref[...]
    for i in range(CONV_W):
        conv = conv + ext_scr[CONV_PAD - n_prev + i:CONV_PAD - n_prev + i + C, :] * cw_ref[i:i + 1, :]

    @pl.when(t == n_t - 1)
    def _():
        convt_ref[0] = ext_scr[CONV_PAD + last_valid - n_prev:CONV_PAD + last_valid, :]

    ext_scr[CONV_PAD - n_prev:CONV_PAD, :] = ext_scr[CONV_PAD + C - n_prev:CONV_PAD + C, :]

    xa = _silu(conv)
    xs = xa[:, 0:D_B]
    bm = xa[:, D_B:D_B + N_GROUPS * D_STATE]
    cm = xa[:, D_B + N_GROUPS * D_STATE:CONV_DIM]

    dtv = _softplus(dt_ref[...] + dtb_ref[...])
    if last_valid < C:
        row = lax.broadcasted_iota(jnp.int32, (C, 1), 0)
        dtv = jnp.where(jnp.logical_or(t < n_t - 1, row < last_valid), dtv, 0.0)
    da = dtv * (-jnp.exp(alog_ref[...]))
    a_cs = _sel_dot(tri_ref[...], da)
    ex = _dot_sel(jnp.concatenate([dtv, da, a_cs], axis=0), e64_ref[...])
    dt_x, da_x, acs_x = ex[0:C], ex[C:2 * C], ex[2 * C:3 * C]
    acs_row = jnp.sum(da_x * mrow_ref[...], axis=0, keepdims=True)
    seg = jnp.where(mtril_ref[...] > 0.0, jnp.exp(acs_x - acs_row), 0.0)

    hpg = H_B // N_GROUPS
    cb_tiles = []
    for g in range(N_GROUPS):
        b_g = bm[:, g * D_STATE:(g + 1) * D_STATE]
        c_g = cm[:, g * D_STATE:(g + 1) * D_STATE]
        cb_tiles.append(_bdot_nt(c_g, jnp.concatenate([b_g] * hpg, axis=0)))
    scores = jnp.concatenate(cb_tiles, axis=1) * seg
    xdt = xs * dt_x
    acs_last = acs_x[C - 1:C, :]
    e_in = jnp.exp(acs_x)
    e_end = jnp.exp(acs_last - acs_x)
    e_all = jnp.exp(acs_last)

    lo = _lane_lo((C, LANES))
    pairs_per_group = N_PAIR_B // N_GROUPS
    pairs = range(N_PAIR_B)
    lanes = [slice(p * LANES, (p + 1) * LANES) for p in pairs]
    b_bf = [bm[:, g * D_STATE:(g + 1) * D_STATE].astype(BF16) for g in range(N_GROUPS)]
    c_bf = [cm[:, g * D_STATE:(g + 1) * D_STATE].astype(BF16) for g in range(N_GROUPS)]
    y_diag = [_bdot(scores[:, lanes[p]], _stack_heads(xdt[:, lanes[p]], lo)) for p in pairs]
    y_off = [_bdot(c_bf[p // pairs_per_group], s_scr[p]) for p in pairs]
    s_add = [_bdot_tn(b_bf[p // pairs_per_group], xdt[:, lanes[p]] * e_end[:, lanes[p]]) for p in pairs]
    ys = []
    for p in pairs:
        sl = lanes[p]
        s_scr[p] = s_scr[p] * e_all[:, sl] + s_add[p]
        y = y_diag[p] + y_off[p] * e_in[:, sl] + dx_ref[:, sl] * xs[:, sl]
        ys.append(y * _silu(z_ref[:, sl]))

    gw = D_B // N_GROUPS
    for g in range(N_GROUPS):
        yg = jnp.concatenate(ys[g * pairs_per_group:(g + 1) * pairs_per_group], axis=1)
        ms = jnp.mean(yg * yg, axis=-1, keepdims=True)
        ob_ref[:, g * gw:(g + 1) * gw] = (
            yg * lax.rsqrt(ms + 1e-5) * nw_ref[:, g * gw:(g + 1) * gw]).astype(ob_ref.dtype)

    @pl.when(t == n_t - 1)
    def _():
        for p in range(N_PAIR_B):
            st_ref[0, p] = s_scr[p].T


def _ssd(xbc, zb, dt, s0, conv0, prm, consts, B, T, t_valid):
    n_t = T // CHUNK
    last_valid = t_valid - (n_t - 1) * CHUNK
    assert CONV_W - 1 <= last_valid <= CHUNK
    row = lambda b, t: (b * n_t + t, 0)
    const = lambda b, t: (0, 0)
    vec = lambda w: pl.BlockSpec((1, w), const)
    kern = functools.partial(_ssd_kernel, n_t=n_t, last_valid=last_valid)
    return pl.pallas_call(
        kern,
        grid=(B, n_t),
        in_specs=[pl.BlockSpec((CHUNK, CONV_DIM), row),
                  pl.BlockSpec((CHUNK, D_B), row),
                  pl.BlockSpec((CHUNK, DT_PAD), row),
                  pl.BlockSpec((1, N_PAIR_B, LANES, D_STATE), lambda b, t: (b, 0, 0, 0)),
                  pl.BlockSpec((1, CONV_W - 1, CONV_DIM), lambda b, t: (b, 0, 0)),
                  pl.BlockSpec((CONV_W, CONV_DIM), const), vec(CONV_DIM),
                  vec(DT_PAD), vec(DT_PAD), vec(D_B), vec(D_B),
                  pl.BlockSpec((CHUNK, CHUNK), const),
                  pl.BlockSpec((DT_PAD, D_B), const),
                  pl.BlockSpec((CHUNK, D_B), const),
                  pl.BlockSpec((CHUNK, D_B), const)],
        out_specs=[pl.BlockSpec((CHUNK, D_B), row),
                   pl.BlockSpec((1, N_PAIR_B, LANES, D_STATE), lambda b, t: (b, 0, 0, 0)),
                   pl.BlockSpec((1, CONV_W - 1, CONV_DIM), lambda b, t: (b, 0, 0))],
        out_shape=[jax.ShapeDtypeStruct((B * T, D_B), BF16),
                   jax.ShapeDtypeStruct((B, N_PAIR_B, LANES, D_STATE), F32),
                   jax.ShapeDtypeStruct((B, CONV_W - 1, CONV_DIM), F32)],
        scratch_shapes=[pltpu.VMEM((N_PAIR_B, D_STATE, LANES), F32),
                        pltpu.VMEM((CONV_PAD + CHUNK, CONV_DIM), F32)],
        compiler_params=_cparams(("parallel", "arbitrary")),
        name="ssd_chunk",
    )(xbc, zb, dt, s0, conv0, prm["conv_w"], prm["conv_b"], prm["dt_bias"], prm["a_log"],
      prm["d_x"], prm["norm_w"], consts["tri"], consts["e64"], consts["mrow"], consts["mtril"])


def _kv_prep_kernel(k_ref, v_ref, w_ref, kall_ref, vall_ref, kn_ref, vb_ref, knew_ref, vnew_ref):
    del kall_ref, vall_ref
    tm = k_ref.shape[0]
    lo = _lane_lo((tm, LANES))
    for p in range(N_PAIR_C):
        sl = slice(p * LANES, (p + 1) * LANES)
        k = k_ref[:, sl]
        ms = _head_sum(k * k, lo) * (1.0 / HEAD_DIM)
        kn = k * lax.rsqrt(ms + 1e-6) * w_ref[:, sl]
        v = v_ref[:, sl]
        kn_ref[:, sl] = kn.astype(kn_ref.dtype)
        vb_ref[:, sl] = v.astype(vb_ref.dtype)
        knt, vt = kn.T, v.T
        for j in range(PAIR):
            h = p * PAIR + j
            knew_ref[0, 0, h] = knt[j * HEAD_DIM:(j + 1) * HEAD_DIM, :]
            vnew_ref[0, 0, h] = vt[j * HEAD_DIM:(j + 1) * HEAD_DIM, :]


def _kv_prep(k, v, knw, B, T, tm, layer, kv_all):
    n_t = T // tm
    row = lambda b, t: (b * n_t + t, 0)
    hm = pl.BlockSpec((1, 1, H_C, HEAD_DIM, tm), lambda b, t: (layer, b, 0, 0, t))
    hm_shape = jax.ShapeDtypeStruct(kv_all[0].shape, F32)
    return pl.pallas_call(
        _kv_prep_kernel,
        grid=(B, n_t),
        in_specs=[pl.BlockSpec((tm, D_C), row), pl.BlockSpec((tm, D_C), row),
                  pl.BlockSpec((1, D_C), lambda b, t: (0, 0)),
                  pl.BlockSpec(memory_space=pl.ANY), pl.BlockSpec(memory_space=pl.ANY)],
        out_specs=[pl.BlockSpec((tm, D_C), row), pl.BlockSpec((tm, D_C), row), hm, hm],
        out_shape=[jax.ShapeDtypeStruct((B * T, D_C), BF16),
                   jax.ShapeDtypeStruct((B * T, D_C), BF16), hm_shape, hm_shape],
        input_output_aliases={3: 2, 4: 3},
        compiler_params=_cparams(("parallel", "parallel")),
        name="sb_kv_prep",
    )(k, v, knw, *kv_all)


KB = LANES


KG = 4
MASKED_LOGIT = -1e30


def _sb_attn_kernel(q_ref, k_ref, v_ref, g_ref, qw_ref, mcat_ref, o_ref, *, tq, q_start, n_q):
    qi = pl.program_id(2)
    gk = KG * KB
    lo = _lane_lo((tq, LANES))
    q = q_ref[...]
    ms = _head_sum(q * q, lo) * (1.0 / HEAD_DIM)
    qn = q * lax.rsqrt(ms + 1e-6) * qw_ref[...] * SB_SCALE
    zero = jnp.zeros_like(qn)
    q_both = jnp.concatenate([jnp.where(lo, qn, zero), jnp.where(lo, zero, qn)], axis=0).astype(BF16)
    q0 = q_start + qi * tq
    g_top = q0 // gk
    bp = 2 * KB
    n_bp_full = KG // 2

    def group(gi, carry, masked, n_bp):
        c_heads, acc = list(carry[:PAIR]), carry[PAIR]
        nk = n_bp * bp
        ks = pl.multiple_of(gi * gk, gk)
        kg = k_ref[pl.ds(ks, nk), :]
        vg = v_ref[pl.ds(ks, nk), :]
        lo_kv = _lane_lo((nk, LANES))
        z = _bdot_nt(q_both, kg)
        if masked:
            k_pos = ks + lax.broadcasted_iota(jnp.int32, (PAIR * tq, nk), 1)
            q_pos = q0 + (lax.broadcasted_iota(jnp.int32, (PAIR * tq, nk), 0) & (tq - 1))
            z = jnp.where(k_pos < q_pos, z, MASKED_LOGIT)
        zb = z.astype(BF16)
        l = jnp.log(1.0 + jnp.exp(-jnp.abs(zb)))
        sp = jnp.maximum(zb, 0.0) + l
        d = jnp.minimum(zb, 0.0) - l
        tiles = [sp[h * tq:(h + 1) * tq, j * bp:(j + 1) * bp] for h in range(PAIR) for j in range(n_bp)]
        tc = jnp.dot(jnp.concatenate(tiles, axis=0), mcat_ref[...], preferred_element_type=F32)
        atts = []
        for h in range(PAIR):
            c = c_heads[h]
            att_h = [None] * n_bp
            for j in reversed(range(n_bp)):
                t0 = (h * n_bp + j) * tq
                rows = slice(h * tq, (h + 1) * tq)
                tail = tc[t0:t0 + tq, :]
                ex = (tail + c).astype(BF16) + d[rows, j * bp:(j + 1) * bp]
                att_h[j] = jnp.exp(ex)
                c = c + (tail[:, 0:1] - sp[rows, j * bp:j * bp + 1].astype(F32))
            c_heads[h] = c
            atts.extend(att_h)
        vz = jnp.zeros_like(vg)
        v_st = jnp.concatenate([jnp.where(lo_kv, vg, vz), jnp.where(lo_kv, vz, vg)], axis=0)
        acc = acc + jnp.dot(jnp.concatenate(atts, axis=1), v_st, preferred_element_type=F32)
        return (*c_heads, acc)

    init = (jnp.zeros((tq, 1), F32),) * PAIR + (jnp.zeros((tq, LANES), F32),)
    if n_q == 1:
        carry = group(g_top, init, True, -(-(q_start % gk + tq) // bp))
    else:
        assert tq == bp and n_bp_full == 2
        carry = lax.cond(q0 % gk == 0, lambda: group(g_top, init, True, 1),
                         lambda: group(g_top, init, True, 2))
    carry = lax.fori_loop(0, g_top, lambda i, c: group(g_top - 1 - i, c, False, n_bp_full), carry)
    o_ref[...] = (carry[PAIR] * _silu(g_ref[...])).astype(o_ref.dtype)


def _sb_attn(q, kn, vb, gc, qw, mcat, B, Tq, Tk, tq, q_start):
    nq = Tq // tq
    gk = KG * KB
    assert gk % tq == 0 and q_start % gk == 0 and tq & (tq - 1) == 0
    assert Tk % (2 * KB) == 0 and Tk >= q_start + Tq
    qrow = lambda b, p, i: (b * nq + i, p)
    kv = lambda b, p, i: (b, p)
    kern = functools.partial(_sb_attn_kernel, tq=tq, q_start=q_start, n_q=nq)
    return pl.pallas_call(
        kern,
        grid=(B, N_PAIR_C, nq),
        in_specs=[pl.BlockSpec((tq, LANES), qrow),
                  pl.BlockSpec((Tk, LANES), kv),
                  pl.BlockSpec((Tk, LANES), kv),
                  pl.BlockSpec((tq, LANES), qrow),
                  pl.BlockSpec((1, LANES), lambda b, p, i: (0, 0)),
                  pl.BlockSpec((2 * KB, 2 * KB), lambda b, p, i: (0, 0))],
        out_specs=pl.BlockSpec((tq, LANES), qrow),
        out_shape=jax.ShapeDtypeStruct((B * Tq, D_C), BF16),
        compiler_params=_cparams(("parallel", "parallel", "arbitrary")),
        name="sb_attn",
    )(q, kn, vb, gc, qw, mcat)


def _out_proj_kernel(x_ref, oa_ref, ob_ref, oc_ref, w_ref, y_ref):
    acc = jnp.dot(oa_ref[...], w_ref[0:D_A, :], preferred_element_type=F32)
    acc = acc + jnp.dot(ob_ref[...], w_ref[D_A:D_A + D_B, :], preferred_element_type=F32)
    acc = acc + jnp.dot(oc_ref[...], w_ref[D_A + D_B:D_MIX, :], preferred_element_type=F32)
    y_ref[...] = x_ref[...] + acc


def _out_proj(x, oa, ob, oc, w_out, tm):
    n = x.shape[0]
    blk = lambda w: pl.BlockSpec((tm, w), lambda i: (i, 0))
    return pl.pallas_call(
        _out_proj_kernel,
        grid=(n // tm,),
        in_specs=[blk(D_MODEL), blk(D_A), blk(D_B), blk(D_C),
                  pl.BlockSpec((D_MIX, D_MODEL), lambda i: (0, 0))],
        out_specs=blk(D_MODEL),
        out_shape=jax.ShapeDtypeStruct((n, D_MODEL), F32),
        compiler_params=_cparams(("parallel",)),
        name="out_proj",
    )(x, oa, ob, oc, w_out)


def _constants():
    i64 = jnp.arange(CHUNK)
    lane = jnp.arange(D_B)
    kj = jnp.arange(KB)
    return {
        "tri": (i64[:, None] >= i64[None, :]).astype(BF16),
        "e64": (jnp.arange(DT_PAD)[:, None] == lane[None, :] // HEAD_DIM).astype(BF16),
        "mrow": (i64[:, None] <= lane[None, :] % CHUNK).astype(F32),
        "mtril": (lane[None, :] % CHUNK <= i64[:, None]).astype(F32),
        "mcat": -(jnp.arange(2 * KB)[:, None] > jnp.arange(2 * KB)[None, :]).astype(BF16),
    }


def _pack_w_in(w):
    o1 = W_SHIFT
    o2 = o1 + D_A
    o3 = o2 + D_B
    o4 = o3 + CONV_DIM
    o5 = o4 + H_B
    dt_cols = jnp.pad(w[:, o4:o5], ((0, 0), (0, DT_PAD - H_B)))
    return jnp.concatenate([w[:, :o4], dt_cols, w[:, o5:]], axis=1).astype(BF16)


def _to_block_diag(s):
    b = s.shape[0]
    s = s.reshape(b, N_PAIR_A, PAIR, HEAD_DIM, HEAD_DIM)
    eye = jnp.eye(PAIR, dtype=s.dtype)
    out = s[:, :, :, :, None, :] * eye[None, None, :, None, :, None]
    return out.reshape(b, N_PAIR_A, LANES, LANES)


def _from_block_diag(sbd):
    b = sbd.shape[0]
    s = sbd.reshape(b, N_PAIR_A, PAIR, HEAD_DIM, PAIR, HEAD_DIM)
    s = jnp.stack([s[:, :, j, :, j, :] for j in range(PAIR)], axis=2)
    return s.reshape(b, H_A, HEAD_DIM, HEAD_DIM)


def _layer_params(l, norm_w, w_in, w_out, rwkv_mu, rwkv_w0, rwkv_w2, rwkv_a0, rwkv_a2, rwkv_k_k,
                  rwkv_k_a, rwkv_r_k, rwkv_ln_w, rwkv_ln_b, ssm_conv_w, ssm_conv_b, ssm_dt_bias,
                  ssm_A_log, ssm_D, ssm_norm_w, sb_q_norm_w, sb_k_norm_w):
    row = lambda x: x.reshape(1, -1)
    pad_h = lambda x: jnp.pad(x, (0, DT_PAD - H_B)).reshape(1, DT_PAD)
    return {
        "norm_w": row(norm_w[l]), "w_in": _pack_w_in(w_in[l]), "w_out": w_out[l].astype(BF16),
        "rwkv": {"mu": row(rwkv_mu[l]), "w0": row(rwkv_w0[l]), "w2": rwkv_w2[l].astype(BF16),
                 "a0": row(rwkv_a0[l]), "a2": rwkv_a2[l].astype(BF16), "k_k": row(rwkv_k_k[l]),
                 "k_a": row(rwkv_k_a[l]), "r_k": row(rwkv_r_k[l]), "ln_w": row(rwkv_ln_w[l]),
                 "ln_b": row(rwkv_ln_b[l])},
        "ssm": {"conv_w": ssm_conv_w[l], "conv_b": row(ssm_conv_b[l]),
                "dt_bias": pad_h(ssm_dt_bias[l]), "a_log": pad_h(ssm_A_log[l]),
                "d_x": row(jnp.repeat(ssm_D[l], HEAD_DIM)), "norm_w": row(ssm_norm_w[l])},
        "sb_qw": row(jnp.tile(sb_q_norm_w[l], PAIR)),
        "sb_kw": row(jnp.tile(sb_k_norm_w[l], H_C)),
    }


def _layer(x, prm, consts, B, T, t_valid, s_rwkv_bd, shift, s_ssm, conv_buf, k_past, v_past,
           layer, kv_all, tm_in, tm_kv, tm_out, tq):
    ua, ga, zb, xbc, dt, qc, kc, vc, gc = _in_proj(x, prm["norm_w"], prm["w_in"], tm_in)
    oa, s_rwkv_bd, shift = _rwkv(ua, ga, s_rwkv_bd, shift, prm["rwkv"], consts["tri"], B, T, t_valid)
    ob, s_ssm, conv_buf = _ssd(xbc, zb, dt, s_ssm, conv_buf, prm["ssm"], consts, B, T, t_valid)
    kn, vb, k_new, v_new = _kv_prep(kc, vc, prm["sb_kw"], B, T, tm_kv, layer, kv_all)
    if k_past is None:
        q_start, t_k = 0, T
    else:
        q_start = k_past.shape[1]
        t_k = -(-(q_start + T) // (2 * KB)) * (2 * KB)
        tail = jnp.zeros((B, t_k - q_start - T, D_C), BF16)
        cat = lambda past, new: jnp.concatenate(
            [past, new.reshape(B, T, D_C), tail], axis=1).reshape(B * t_k, D_C)
        kn, vb = cat(k_past, kn), cat(v_past, vb)
    oc = _sb_attn(qc, kn, vb, gc, prm["sb_qw"], consts["mcat"], B, T, t_k, tq, q_start)
    y = _out_proj(x, oa, ob, oc, prm["w_out"], tm_out)
    return y, (s_rwkv_bd, shift, s_ssm, conv_buf, k_new, v_new)


def _cache_pack_kernel(c_ref, o_ref):
    for p in range(N_PAIR_C):
        pair = jnp.concatenate([c_ref[0, p * PAIR + j] for j in range(PAIR)], axis=0)
        o_ref[0, :, p * LANES:(p + 1) * LANES] = pair.T.astype(o_ref.dtype)


def _packed_cache(c):
    b, h, p, d = c.shape
    tp = min(p, 512)
    assert p % tp == 0 and (h, d) == (H_C, HEAD_DIM)
    return pl.pallas_call(
        _cache_pack_kernel,
        grid=(b, p // tp),
        in_specs=[pl.BlockSpec((1, h, d, tp), lambda i, j: (i, 0, 0, j))],
        out_specs=pl.BlockSpec((1, tp, h * d), lambda i, j: (i, j, 0)),
        out_shape=jax.ShapeDtypeStruct((b, p, h * d), BF16),
        compiler_params=_cparams(("parallel", "parallel")),
        name="sb_cache_pack",
    )(jnp.swapaxes(c, -1, -2))


def kernel(x_prompt, x_sample, state_rwkv, state_rwkv_shift, state_ssm, state_conv, cache_sb_k, cache_sb_v,
           norm_w, w_in, w_out, rwkv_mu, rwkv_w0, rwkv_w2, rwkv_a0, rwkv_a2, rwkv_k_k, rwkv_k_a, rwkv_r_k,
           rwkv_ln_w, rwkv_ln_b, ssm_conv_w, ssm_conv_b, ssm_dt_bias, ssm_A_log, ssm_D, ssm_norm_w,
           sb_q_norm_w, sb_k_norm_w):
    bp, tp, _ = x_prompt.shape
    bs, ts, _ = x_sample.shape
    depth = w_in.shape[0]
    ts_pad = -(-ts // CHUNK) * CHUNK
    consts = _constants()

    yp = x_prompt.reshape(bp * tp, D_MODEL)
    ys = jnp.pad(x_sample, ((0, 0), (0, ts_pad - ts), (0, 0))).reshape(bs * ts_pad, D_MODEL)

    new_p = [[] for _ in range(4)]
    new_s = [[] for _ in range(4)]
    kv_p = (jnp.zeros((depth, bp, H_C, HEAD_DIM, tp), F32),) * 2
    kv_s = (jnp.zeros((depth, bs, H_C, HEAD_DIM, ts_pad), F32),) * 2
    for l in range(depth):
        prm = _layer_params(l, norm_w, w_in, w_out, rwkv_mu, rwkv_w0, rwkv_w2, rwkv_a0, rwkv_a2,
                            rwkv_k_k, rwkv_k_a, rwkv_r_k, rwkv_ln_w, rwkv_ln_b, ssm_conv_w, ssm_conv_b,
                            ssm_dt_bias, ssm_A_log, ssm_D, ssm_norm_w, sb_q_norm_w, sb_k_norm_w)
        yp, st_p = _layer(
            yp, prm, consts, bp, tp, tp,
            jnp.zeros((bp, N_PAIR_A, LANES, LANES), F32), jnp.zeros((bp, 1, W_SHIFT), F32),
            jnp.zeros((bp, N_PAIR_B, LANES, D_STATE), F32), jnp.zeros((bp, CONV_W - 1, CONV_DIM), F32),
            None, None, l, kv_p, tm_in=256, tm_kv=256, tm_out=512, tq=2 * KB)
        ys, st_s = _layer(
            ys, prm, consts, bs, ts_pad, ts,
            _to_block_diag(state_rwkv[l]), state_rwkv_shift[l],
            state_ssm[l].reshape(bs, N_PAIR_B, LANES, D_STATE), state_conv[l],
            _packed_cache(cache_sb_k[l]), _packed_cache(cache_sb_v[l]),
            l, kv_s, tm_in=256, tm_kv=ts_pad, tm_out=512, tq=ts_pad)
        kv_p, kv_s = st_p[4:], st_s[4:]
        for i in range(4):
            new_p[i].append(st_p[i])
            new_s[i].append(st_s[i])

    def finish(y, st, kv_all, b, t_pad, t):
        s_rwkv, shift, s_ssm, conv = [jnp.stack(v) for v in st]
        k_new, v_new = [jnp.swapaxes(a, -1, -2) for a in kv_all]
        return (y.reshape(b, t_pad, D_MODEL)[:, :t],
                (_from_block_diag(s_rwkv.reshape(depth * b, N_PAIR_A, LANES, LANES))
                 .reshape(depth, b, H_A, HEAD_DIM, HEAD_DIM)),
                shift,
                s_ssm.reshape(depth, b, H_B, HEAD_DIM, D_STATE),
                conv,
                k_new[:, :, :, :t],
                v_new[:, :, :, :t])

    yp, *rest_p = finish(yp, new_p, kv_p, bp, tp, tp)
    ys, *rest_s = finish(ys, new_s, kv_s, bs, ts_pad, ts)
    return (yp, ys, *rest_p, *rest_s)
```

```python
import functools

import jax
import jax.numpy as jnp
from jax import lax
from jax.experimental import pallas as pl
from jax.experimental.pallas import tpu as pltpu

F32 = jnp.float32
BF16 = jnp.bfloat16

D_MODEL = 1024
D_MIX = 2 * D_MODEL
HEAD_DIM = 64
D_A = 768
H_A = D_A // HEAD_DIM
R_W = 64
R_A = 64
GN_EPS = 64e-5
D_B = 768
H_B = D_B // HEAD_DIM
N_GROUPS = 2
D_STATE = 128
CONV_W = 4
CONV_DIM = D_B + 2 * N_GROUPS * D_STATE
D_C = 512
H_C = D_C // HEAD_DIM
W_SHIFT = 3 * D_A + R_W + R_A
SB_SCALE = HEAD_DIM ** -0.5

LANES = 128
PAIR = LANES // HEAD_DIM
CHUNK = 64
VMEM_LIMIT = 56 * 1024 * 1024

N_PAIR_A = H_A // PAIR
N_PAIR_B = H_B // PAIR
N_PAIR_C = H_C // PAIR
DT_PAD = LANES

IN_WIDTHS = (W_SHIFT, D_A, D_B, CONV_DIM, DT_PAD, D_C, D_C, D_C, D_C)


def _cparams(sem):
    return pltpu.CompilerParams(dimension_semantics=sem, vmem_limit_bytes=VMEM_LIMIT)


def _bdot(a, b):
    return jnp.dot(a.astype(BF16), b.astype(BF16), preferred_element_type=F32)


def _bdot_nt(a, b):
    return lax.dot_general(a.astype(BF16), b.astype(BF16), (((1,), (1,)), ((), ())),
                           preferred_element_type=F32)


def _bdot_tn(a, b):
    return lax.dot_general(a.astype(BF16), b.astype(BF16), (((0,), (0,)), ((), ())),
                           preferred_element_type=F32)


def _split_bf16(x, n):
    parts, r = [], x
    for i in range(n):
        p = r.astype(BF16)
        parts.append(p)
        if i + 1 < n:
            r = r - p.astype(F32)
    return parts


def _sel_dot(sel, x, n=3):
    acc = None
    for p in _split_bf16(x, n):
        d = jnp.dot(sel, p, preferred_element_type=F32)
        acc = d if acc is None else acc + d
    return acc


def _dot_sel(x, sel, n=3):
    acc = None
    for p in _split_bf16(x, n):
        d = jnp.dot(p, sel, preferred_element_type=F32)
        acc = d if acc is None else acc + d
    return acc


def _lane_lo(shape):
    return lax.broadcasted_iota(jnp.int32, shape, len(shape) - 1) < HEAD_DIM


def _head_sum(x, lo):
    s_lo = jnp.sum(jnp.where(lo, x, 0.0), axis=-1, keepdims=True)
    s_hi = jnp.sum(jnp.where(lo, 0.0, x), axis=-1, keepdims=True)
    return jnp.where(lo, s_lo, s_hi)


def _stack_heads(x, lo):
    zero = jnp.zeros_like(x)
    return jnp.concatenate([jnp.where(lo, x, zero), jnp.where(lo, zero, x)], axis=0)


def _silu(x):
    return x * jax.nn.sigmoid(x)


def _softplus(x):
    return jnp.maximum(x, 0.0) + jnp.log1p(jnp.exp(-jnp.abs(x)))


def _in_proj_kernel(x_ref, nw_ref, w_ref, *out_refs):
    x = x_ref[...]
    ms = jnp.mean(x * x, axis=-1, keepdims=True)
    h = (x * lax.rsqrt(ms + 1e-6) * nw_ref[...]).astype(BF16)
    off = 0
    for o_ref, wd in zip(out_refs, IN_WIDTHS):
        o_ref[...] = jnp.dot(h, w_ref[:, off:off + wd], preferred_element_type=F32)
        off += wd


def _in_proj(x, norm_w, w_cat, tm):
    n = x.shape[0]
    n_cols = sum(IN_WIDTHS)
    return pl.pallas_call(
        _in_proj_kernel,
        grid=(n // tm,),
        in_specs=[pl.BlockSpec((tm, D_MODEL), lambda i: (i, 0)),
                  pl.BlockSpec((1, D_MODEL), lambda i: (0, 0)),
                  pl.BlockSpec((D_MODEL, n_cols), lambda i: (0, 0))],
        out_specs=[pl.BlockSpec((tm, wd), lambda i: (i, 0)) for wd in IN_WIDTHS],
        out_shape=[jax.ShapeDtypeStruct((n, wd), F32) for wd in IN_WIDTHS],
        compiler_params=_cparams(("parallel",)),
        name="in_proj",
    )(x, norm_w, w_cat)


def _rwkv_kernel(ua_ref, ga_ref, s0_ref, sh0_ref, mu_ref, w0_ref, w2_ref, a0_ref, a2_ref,
                 kkw_ref, kaw_ref, rkw_ref, lnw_ref, lnb_ref, tri_ref,
                 oa_ref, st_ref, sht_ref,
                 s_scr, prev_scr, us_scr, *, n_t, last_valid):
    C = CHUNK
    t = pl.program_id(1)

    @pl.when(t == 0)
    def _():
        s_scr[...] = s0_ref[0]
        prev_scr[...] = sh0_ref[0]

    u = ua_ref[...]
    row = lax.broadcasted_iota(jnp.int32, (C, 1), 0)
    u_prev = jnp.where(row == 0, prev_scr[...], pltpu.roll(u, 1, axis=0))
    us_scr[...] = u + (u_prev - u) * mu_ref[...]
    prev_scr[...] = u[C - 1:C, :]

    padded = last_valid < C
    valid = jnp.logical_or(t < n_t - 1, row < last_valid) if padded else None

    w_lo = us_scr[:, 3 * D_A:3 * D_A + R_W]
    a_lo = us_scr[:, 3 * D_A + R_W:W_SHIFT]
    wl = w0_ref[...] + _bdot(jnp.tanh(w_lo), w2_ref[...])
    lw = -jnp.exp(-_softplus(-wl) - 0.5)
    a_all = jax.nn.sigmoid(a0_ref[...] + _bdot(a_lo, a2_ref[...]))
    if padded:
        lw = jnp.where(valid, lw, 0.0)
    cum_all = _sel_dot(tri_ref[...], lw)

    lo = _lane_lo((C, LANES))
    ri = lax.broadcasted_iota(jnp.int32, (2 * C, 2 * C), 0) & (C - 1)
    ci = lax.broadcasted_iota(jnp.int32, (2 * C, 2 * C), 1) & (C - 1)
    strict = ci < ri
    incl = ci <= ri

    pairs = range(N_PAIR_A)
    lanes = [slice(p * LANES, (p + 1) * LANES) for p in pairs]
    lhs, rhs, ends, vs, w_all, extra = [], [], [], [], [], []
    for p in pairs:
        sl = lanes[p]
        r = us_scr[:, p * LANES:(p + 1) * LANES]
        k = us_scr[:, D_A + p * LANES:D_A + (p + 1) * LANES]
        v = us_scr[:, 2 * D_A + p * LANES:2 * D_A + (p + 1) * LANES]
        a = a_all[:, sl]
        cum = cum_all[:, sl]
        kk = k * kkw_ref[:, sl]
        kkn = kk / jnp.maximum(jnp.sqrt(_head_sum(kk * kk, lo)), 1e-12)
        k2 = k * (1.0 + (a - 1.0) * kaw_ref[:, sl])
        vv = v
        if padded:
            kkn = jnp.where(valid, kkn, 0.0)
            k2 = jnp.where(valid, k2, 0.0)
            vv = jnp.where(valid, v, 0.0)
        cum_last = cum[C - 1:C, :]
        w_inc = jnp.exp(cum)
        w_exc = jnp.exp(cum - lw[:, sl])
        w_inv = jnp.exp(-cum)
        w_end = jnp.exp(cum_last - cum)
        w_all.append(jnp.exp(cum_last))
        ad = kkn * a
        lhs.append(jnp.concatenate(
            [_stack_heads(-kkn * w_exc, lo), _stack_heads(r * w_inc, lo)], axis=0).astype(BF16))
        rhs.append(jnp.concatenate(
            [_stack_heads(ad * w_inv, lo), _stack_heads(k2 * w_inv, lo)], axis=0).astype(BF16))
        ends.append(jnp.concatenate(
            [_stack_heads(ad * w_end, lo), _stack_heads(k2 * w_end, lo)], axis=0).astype(BF16))
        vs.append(_stack_heads(vv, lo).astype(BF16))
        bonus = _head_sum(r * k2 * rkw_ref[:, sl], lo)
        extra.append((bonus * v, _silu(ga_ref[:, sl])))

    g = [_bdot_nt(lhs[p], rhs[p]) for p in pairs]
    ps = [_bdot_nt(lhs[p], s_scr[p]) for p in pairs]
    x = [ps[p][0:2 * C] + _bdot(jnp.where(strict, g[p][0:2 * C, 2 * C:4 * C], 0.0), vs[p])
         for p in pairs]
    apow = [jnp.where(strict, g[p][0:2 * C, 0:2 * C], 0.0).astype(BF16) for p in pairs]
    n_sq = C.bit_length() - 1
    for i in range(n_sq):
        x = [x[p] + _bdot(apow[p], x[p]) for p in pairs]
        if i + 1 < n_sq:
            apow = [_bdot(apow[p], apow[p]).astype(BF16) for p in pairs]
    for p in pairs:
        sl = lanes[p]
        b_ab = jnp.where(incl, g[p][2 * C:4 * C, 0:2 * C], 0.0).astype(BF16)
        b_ak = jnp.where(incl, g[p][2 * C:4 * C, 2 * C:4 * C], 0.0).astype(BF16)
        uv = jnp.concatenate([x[p].astype(BF16), vs[p]], axis=0)
        ys = ps[p][2 * C:4 * C] + _bdot(jnp.concatenate([b_ab, b_ak], axis=1), uv)
        y = ys[0:C] + ys[C:2 * C]
        s_scr[p] = s_scr[p] * w_all[p] + _bdot_tn(uv, ends[p])

        mean = _head_sum(y, lo) * (1.0 / HEAD_DIM)
        d = y - mean
        var = _head_sum(d * d, lo) * (1.0 / HEAD_DIM)
        yn = d * lax.rsqrt(var + GN_EPS) * lnw_ref[:, sl] + lnb_ref[:, sl]
        bv, gate = extra[p]
        oa_ref[:, sl] = ((yn + bv) * gate).astype(oa_ref.dtype)

    @pl.when(t == n_t - 1)
    def _():
        st_ref[0] = s_scr[...]
        sht_ref[0] = ua_ref[last_valid - 1:last_valid, :]


def _rwkv(ua, ga, s0_bd, sh0, prm, tri, B, T, t_valid):
    n_t = T // CHUNK
    last_valid = t_valid - (n_t - 1) * CHUNK
    assert 1 <= last_valid <= CHUNK
    row = lambda b, t: (b * n_t + t, 0)
    const = lambda b, t: (0, 0)
    vec = lambda w: pl.BlockSpec((1, w), const)
    kern = functools.partial(_rwkv_kernel, n_t=n_t, last_valid=last_valid)
    return pl.pallas_call(
        kern,
        grid=(B, n_t),
        in_specs=[pl.BlockSpec((CHUNK, W_SHIFT), row),
                  pl.BlockSpec((CHUNK, D_A), row),
                  pl.BlockSpec((1, N_PAIR_A, LANES, LANES), lambda b, t: (b, 0, 0, 0)),
                  pl.BlockSpec((1, 1, W_SHIFT), lambda b, t: (b, 0, 0)),
                  vec(W_SHIFT), vec(D_A), pl.BlockSpec((R_W, D_A), const),
                  vec(D_A), pl.BlockSpec((R_A, D_A), const),
                  vec(D_A), vec(D_A), vec(D_A), vec(D_A), vec(D_A),
                  pl.BlockSpec((CHUNK, CHUNK), const)],
        out_specs=[pl.BlockSpec((CHUNK, D_A), row),
                   pl.BlockSpec((1, N_PAIR_A, LANES, LANES), lambda b, t: (b, 0, 0, 0)),
                   pl.BlockSpec((1, 1, W_SHIFT), lambda b, t: (b, 0, 0))],
        out_shape=[jax.ShapeDtypeStruct((B * T, D_A), BF16),
                   jax.ShapeDtypeStruct((B, N_PAIR_A, LANES, LANES), F32),
                   jax.ShapeDtypeStruct((B, 1, W_SHIFT), F32)],
        scratch_shapes=[pltpu.VMEM((N_PAIR_A, LANES, LANES), F32),
                        pltpu.VMEM((1, W_SHIFT), F32),
                        pltpu.VMEM((CHUNK, W_SHIFT), F32)],
        compiler_params=_cparams(("parallel", "arbitrary")),
        name="rwkv7_chunk",
    )(ua, ga, s0_bd, sh0, prm["mu"], prm["w0"], prm["w2"], prm["a0"], prm["a2"],
      prm["k_k"], prm["k_a"], prm["r_k"], prm["ln_w"], prm["ln_b"], tri)


CONV_PAD = 8


def _ssd_kernel(xbc_ref, z_ref, dt_ref, s0_ref, conv0_ref, cw_ref, cb_ref, dtb_ref, alog_ref,
                dx_ref, nw_ref, tri_ref, e64_ref, mrow_ref, mtril_ref,
                ob_ref, st_ref, convt_ref,
                s_scr, ext_scr, *, n_t, last_valid):
    C = CHUNK
    t = pl.program_id(1)
    n_prev = CONV_W - 1

    @pl.when(t == 0)
    def _():
        for p in range(N_PAIR_B):
            s_scr[p] = s0_ref[0, p].T
        ext_scr[CONV_PAD - n_prev:CONV_PAD, :] = conv0_ref[0]

    ext_scr[CONV_PAD:CONV_PAD + C, :] = xbc_ref[...]
    conv = cb_ref[...]
    for i in range(CONV_W):
        conv = conv + ext_scr[CONV_PAD - n_prev + i:CONV_PAD - n_prev + i + C, :] * cw_ref[i:i + 1, :]

    @pl.when(t == n_t - 1)
    def _():
        convt_ref[0] = ext_scr[CONV_PAD + last_valid - n_prev:CONV_PAD + last_valid, :]

    ext_scr[CONV_PAD - n_prev:CONV_PAD, :] = ext_scr[CONV_PAD + C - n_prev:CONV_PAD + C, :]

    xa = _silu(conv)
    xs = xa[:, 0:D_B]
    bm = xa[:, D_B:D_B + N_GROUPS * D_STATE]
    cm = xa[:, D_B + N_GROUPS * D_STATE:CONV_DIM]

    dtv = _softplus(dt_ref[...] + dtb_ref[...])
    if last_valid < C:
        row = lax.broadcasted_iota(jnp.int32, (C, 1), 0)
        dtv = jnp.where(jnp.logical_or(t < n_t - 1, row < last_valid), dtv, 0.0)
    da = dtv * (-jnp.exp(alog_ref[...]))
    a_cs = _sel_dot(tri_ref[...], da)
    ex = _dot_sel(jnp.concatenate([dtv, da, a_cs], axis=0), e64_ref[...])
    dt_x, da_x, acs_x = ex[0:C], ex[C:2 * C], ex[2 * C:3 * C]
    acs_row = jnp.sum(da_x * mrow_ref[...], axis=0, keepdims=True)
    seg = jnp.where(mtril_ref[...] > 0.0, jnp.exp(acs_x - acs_row), 0.0)

    hpg = H_B // N_GROUPS
    cb_tiles = []
    for g in range(N_GROUPS):
        b_g = bm[:, g * D_STATE:(g + 1) * D_STATE]
        c_g = cm[:, g * D_STATE:(g + 1) * D_STATE]
        cb_tiles.append(_bdot_nt(c_g, jnp.concatenate([b_g] * hpg, axis=0)))
    scores = jnp.concatenate(cb_tiles, axis=1) * seg
    xdt = xs * dt_x
    acs_last = acs_x[C - 1:C, :]
    e_in = jnp.exp(acs_x)
    e_end = jnp.exp(acs_last - acs_x)
    e_all = jnp.exp(acs_last)

    lo = _lane_lo((C, LANES))
    pairs_per_group = N_PAIR_B // N_GROUPS
    pairs = range(N_PAIR_B)
    lanes = [slice(p * LANES, (p + 1) * LANES) for p in pairs]
    b_bf = [bm[:, g * D_STATE:(g + 1) * D_STATE].astype(BF16) for g in range(N_GROUPS)]
    c_bf = [cm[:, g * D_STATE:(g + 1) * D_STATE].astype(BF16) for g in range(N_GROUPS)]
    y_diag = [_bdot(scores[:, lanes[p]], _stack_heads(xdt[:, lanes[p]], lo)) for p in pairs]
    y_off = [_bdot(c_bf[p // pairs_per_group], s_scr[p]) for p in pairs]
    s_add = [_bdot_tn(b_bf[p // pairs_per_group], xdt[:, lanes[p]] * e_end[:, lanes[p]]) for p in pairs]
    ys = []
    for p in pairs:
        sl = lanes[p]
        s_scr[p] = s_scr[p] * e_all[:, sl] + s_add[p]
        y = y_diag[p] + y_off[p] * e_in[:, sl] + dx_ref[:, sl] * xs[:, sl]
        ys.append(y * _silu(z_ref[:, sl]))

    gw = D_B // N_GROUPS
    for g in range(N_GROUPS):
        yg = jnp.concatenate(ys[g * pairs_per_group:(g + 1) * pairs_per_group], axis=1)
        ms = jnp.mean(yg * yg, axis=-1, keepdims=True)
        ob_ref[:, g * gw:(g + 1) * gw] = (
            yg * lax.rsqrt(ms + 1e-5) * nw_ref[:, g * gw:(g + 1) * gw]).astype(ob_ref.dtype)

    @pl.when(t == n_t - 1)
    def _():
        for p in range(N_PAIR_B):
            st_ref[0, p] = s_scr[p].T


def _ssd(xbc, zb, dt, s0, conv0, prm, consts, B, T, t_valid):
    n_t = T // CHUNK
    last_valid = t_valid - (n_t - 1) * CHUNK
    assert CONV_W - 1 <= last_valid <= CHUNK
    row = lambda b, t: (b * n_t + t, 0)
    const = lambda b, t: (0, 0)
    vec = lambda w: pl.BlockSpec((1, w), const)
    kern = functools.partial(_ssd_kernel, n_t=n_t, last_valid=last_valid)
    return pl.pallas_call(
        kern,
        grid=(B, n_t),
        in_specs=[pl.BlockSpec((CHUNK, CONV_DIM), row),
                  pl.BlockSpec((CHUNK, D_B), row),
                  pl.BlockSpec((CHUNK, DT_PAD), row),
                  pl.BlockSpec((1, N_PAIR_B, LANES, D_STATE), lambda b, t: (b, 0, 0, 0)),
                  pl.BlockSpec((1, CONV_W - 1, CONV_DIM), lambda b, t: (b, 0, 0)),
                  pl.BlockSpec((CONV_W, CONV_DIM), const), vec(CONV_DIM),
                  vec(DT_PAD), vec(DT_PAD), vec(D_B), vec(D_B),
                  pl.BlockSpec((CHUNK, CHUNK), const),
                  pl.BlockSpec((DT_PAD, D_B), const),
                  pl.BlockSpec((CHUNK, D_B), const),
                  pl.BlockSpec((CHUNK, D_B), const)],
        out_specs=[pl.BlockSpec((CHUNK, D_B), row),
                   pl.BlockSpec((1, N_PAIR_B, LANES, D_STATE), lambda b, t: (b, 0, 0, 0)),
                   pl.BlockSpec((1, CONV_W - 1, CONV_DIM), lambda b, t: (b, 0, 0))],
        out_shape=[jax.ShapeDtypeStruct((B * T, D_B), BF16),
                   jax.ShapeDtypeStruct((B, N_PAIR_B, LANES, D_STATE), F32),
                   jax.ShapeDtypeStruct((B, CONV_W - 1, CONV_DIM), F32)],
        scratch_shapes=[pltpu.VMEM((N_PAIR_B, D_STATE, LANES), F32),
                        pltpu.VMEM((CONV_PAD + CHUNK, CONV_DIM), F32)],
        compiler_params=_cparams(("parallel", "arbitrary")),
        name="ssd_chunk",
    )(xbc, zb, dt, s0, conv0, prm["conv_w"], prm["conv_b"], prm["dt_bias"], prm["a_log"],
      prm["d_x"], prm["norm_w"], consts["tri"], consts["e64"], consts["mrow"], consts["mtril"])


def _kv_prep_kernel(k_ref, v_ref, w_ref, kall_ref, vall_ref, kn_ref, vb_ref, knew_ref, vnew_ref):
    del kall_ref, vall_ref
    tm = k_ref.shape[0]
    lo = _lane_lo((tm, LANES))
    for p in range(N_PAIR_C):
        sl = slice(p * LANES, (p + 1) * LANES)
        k = k_ref[:, sl]
        ms = _head_sum(k * k, lo) * (1.0 / HEAD_DIM)
        kn = k * lax.rsqrt(ms + 1e-6) * w_ref[:, sl]
        v = v_ref[:, sl]
        kn_ref[:, sl] = kn.astype(kn_ref.dtype)
        vb_ref[:, sl] = v.astype(vb_ref.dtype)
        knt, vt = kn.T, v.T
        for j in range(PAIR):
            h = p * PAIR + j
            knew_ref[0, 0, h] = knt[j * HEAD_DIM:(j + 1) * HEAD_DIM, :]
            vnew_ref[0, 0, h] = vt[j * HEAD_DIM:(j + 1) * HEAD_DIM, :]


def _kv_prep(k, v, knw, B, T, tm, layer, kv_all):
    n_t = T // tm
    row = lambda b, t: (b * n_t + t, 0)
    hm = pl.BlockSpec((1, 1, H_C, HEAD_DIM, tm), lambda b, t: (layer, b, 0, 0, t))
    hm_shape = jax.ShapeDtypeStruct(kv_all[0].shape, F32)
    return pl.pallas_call(
        _kv_prep_kernel,
        grid=(B, n_t),
        in_specs=[pl.BlockSpec((tm, D_C), row), pl.BlockSpec((tm, D_C), row),
                  pl.BlockSpec((1, D_C), lambda b, t: (0, 0)),
                  pl.BlockSpec(memory_space=pl.ANY), pl.BlockSpec(memory_space=pl.ANY)],
        out_specs=[pl.BlockSpec((tm, D_C), row), pl.BlockSpec((tm, D_C), row), hm, hm],
        out_shape=[jax.ShapeDtypeStruct((B * T, D_C), BF16),
                   jax.ShapeDtypeStruct((B * T, D_C), BF16), hm_shape, hm_shape],
        input_output_aliases={3: 2, 4: 3},
        compiler_params=_cparams(("parallel", "parallel")),
        name="sb_kv_prep",
    )(k, v, knw, *kv_all)


KB = LANES


KG = 4
MASKED_LOGIT = -1e30


def _sb_attn_kernel(q_ref, k_ref, v_ref, g_ref, qw_ref, mcat_ref, o_ref, *, tq, q_start, n_q):
    qi = pl.program_id(2)
    gk = KG * KB
    lo = _lane_lo((tq, LANES))
    q = q_ref[...]
    ms = _head_sum(q * q, lo) * (1.0 / HEAD_DIM)
    qn = q * lax.rsqrt(ms + 1e-6) * qw_ref[...] * SB_SCALE
    zero = jnp.zeros_like(qn)
    q_both = jnp.concatenate([jnp.where(lo, qn, zero), jnp.where(lo, zero, qn)], axis=0).astype(BF16)
    q0 = q_start + qi * tq
    g_top = q0 // gk
    bp = 2 * KB
    n_bp_full = KG // 2

    def group(gi, carry, masked, n_bp):
        c_heads, acc = list(carry[:PAIR]), carry[PAIR]
        nk = n_bp * bp
        ks = pl.multiple_of(gi * gk, gk)
        kg = k_ref[pl.ds(ks, nk), :]
        vg = v_ref[pl.ds(ks, nk), :]
        lo_kv = _lane_lo((nk, LANES))
        z = _bdot_nt(q_both, kg)
        if masked:
            k_pos = ks + lax.broadcasted_iota(jnp.int32, (PAIR * tq, nk), 1)
            q_pos = q0 + (lax.broadcasted_iota(jnp.int32, (PAIR * tq, nk), 0) & (tq - 1))
            z = jnp.where(k_pos < q_pos, z, MASKED_LOGIT)
        zb = z.astype(BF16)
        l = jnp.log(1.0 + jnp.exp(-jnp.abs(zb)))
        sp = jnp.maximum(zb, 0.0) + l
        d = jnp.minimum(zb, 0.0) - l
        tiles = [sp[h * tq:(h + 1) * tq, j * bp:(j + 1) * bp] for h in range(PAIR) for j in range(n_bp)]
        tc = jnp.dot(jnp.concatenate(tiles, axis=0), mcat_ref[...], preferred_element_type=F32)
        atts = []
        for h in range(PAIR):
            c = c_heads[h]
            att_h = [None] * n_bp
            for j in reversed(range(n_bp)):
                t0 = (h * n_bp + j) * tq
                rows = slice(h * tq, (h + 1) * tq)
                tail = tc[t0:t0 + tq, :]
                ex = (tail + c).astype(BF16) + d[rows, j * bp:(j + 1) * bp]
                att_h[j] = jnp.exp(ex)
                c = c + (tail[:, 0:1] - sp[rows, j * bp:j * bp + 1].astype(F32))
            c_heads[h] = c
            atts.extend(att_h)
        vz = jnp.zeros_like(vg)
        v_st = jnp.concatenate([jnp.where(lo_kv, vg, vz), jnp.where(lo_kv, vz, vg)], axis=0)
        acc = acc + jnp.dot(jnp.concatenate(atts, axis=1), v_st, preferred_element_type=F32)
        return (*c_heads, acc)

    init = (jnp.zeros((tq, 1), F32),) * PAIR + (jnp.zeros((tq, LANES), F32),)
    assert n_q == 1 or tq == gk
    carry = group(g_top, init, True, -(-(q_start % gk + tq) // bp))
    carry = lax.fori_loop(0, g_top, lambda i, c: group(g_top - 1 - i, c, False, n_bp_full), carry)
    o_ref[...] = (carry[PAIR] * _silu(g_ref[...])).astype(o_ref.dtype)


def _sb_attn(q, kn, vb, gc, qw, mcat, B, Tq, Tk, tq, q_start):
    nq = Tq // tq
    gk = KG * KB
    assert gk % tq == 0 and q_start % gk == 0 and tq & (tq - 1) == 0
    assert Tk % (2 * KB) == 0 and Tk >= q_start + Tq
    qrow = lambda b, p, i: (b * nq + i, p)
    kv = lambda b, p, i: (b, p)
    kern = functools.partial(_sb_attn_kernel, tq=tq, q_start=q_start, n_q=nq)
    return pl.pallas_call(
        kern,
        grid=(B, N_PAIR_C, nq),
        in_specs=[pl.BlockSpec((tq, LANES), qrow),
                  pl.BlockSpec((Tk, LANES), kv),
                  pl.BlockSpec((Tk, LANES), kv),
                  pl.BlockSpec((tq, LANES), qrow),
                  pl.BlockSpec((1, LANES), lambda b, p, i: (0, 0)),
                  pl.BlockSpec((2 * KB, 2 * KB), lambda b, p, i: (0, 0))],
        out_specs=pl.BlockSpec((tq, LANES), qrow),
        out_shape=jax.ShapeDtypeStruct((B * Tq, D_C), BF16),
        compiler_params=_cparams(("parallel", "parallel", "arbitrary")),
        name="sb_attn",
    )(q, kn, vb, gc, qw, mcat)


def _out_proj_kernel(x_ref, oa_ref, ob_ref, oc_ref, w_ref, y_ref):
    acc = jnp.dot(oa_ref[...], w_ref[0:D_A, :], preferred_element_type=F32)
    acc = acc + jnp.dot(ob_ref[...], w_ref[D_A:D_A + D_B, :], preferred_element_type=F32)
    acc = acc + jnp.dot(oc_ref[...], w_ref[D_A + D_B:D_MIX, :], preferred_element_type=F32)
    y_ref[...] = x_ref[...] + acc


def _out_proj(x, oa, ob, oc, w_out, tm):
    n = x.shape[0]
    blk = lambda w: pl.BlockSpec((tm, w), lambda i: (i, 0))
    return pl.pallas_call(
        _out_proj_kernel,
        grid=(n // tm,),
        in_specs=[blk(D_MODEL), blk(D_A), blk(D_B), blk(D_C),
                  pl.BlockSpec((D_MIX, D_MODEL), lambda i: (0, 0))],
        out_specs=blk(D_MODEL),
        out_shape=jax.ShapeDtypeStruct((n, D_MODEL), F32),
        compiler_params=_cparams(("parallel",)),
        name="out_proj",
    )(x, oa, ob, oc, w_out)


def _constants():
    i64 = jnp.arange(CHUNK)
    lane = jnp.arange(D_B)
    return {
        "tri": (i64[:, None] >= i64[None, :]).astype(BF16),
        "e64": (jnp.arange(DT_PAD)[:, None] == lane[None, :] // HEAD_DIM).astype(BF16),
        "mrow": (i64[:, None] <= lane[None, :] % CHUNK).astype(F32),
        "mtril": (lane[None, :] % CHUNK <= i64[:, None]).astype(F32),
        "mcat": -(jnp.arange(2 * KB)[:, None] > jnp.arange(2 * KB)[None, :]).astype(BF16),
    }


N_IN = W_SHIFT + D_A + D_B + CONV_DIM + H_B + 4 * D_C
DT_OFF = W_SHIFT + D_A + D_B + CONV_DIM


def _pack_w_in_kernel(w_ref, o_ref):
    o_ref[:, 0:DT_OFF] = w_ref[0, :, 0:DT_OFF].astype(o_ref.dtype)
    dt_tile = w_ref[0, :, DT_OFF:DT_OFF + DT_PAD]
    lane = lax.broadcasted_iota(jnp.int32, dt_tile.shape, 1)
    o_ref[:, DT_OFF:DT_OFF + DT_PAD] = jnp.where(lane < H_B, dt_tile, 0.0).astype(o_ref.dtype)
    for t in range(4 * D_C // LANES):
        src = DT_OFF + H_B + t * LANES
        dst = DT_OFF + DT_PAD + t * LANES
        o_ref[:, dst:dst + LANES] = w_ref[0, :, src:src + LANES].astype(o_ref.dtype)


def _pack_w_in(w_in, layer):
    assert w_in.shape[1:] == (D_MODEL, N_IN) and DT_OFF % LANES == 0
    tr = 128
    n_cols = sum(IN_WIDTHS)
    return pl.pallas_call(
        _pack_w_in_kernel,
        grid=(D_MODEL // tr,),
        in_specs=[pl.BlockSpec((1, tr, N_IN), lambda i: (layer, i, 0))],
        out_specs=pl.BlockSpec((tr, n_cols), lambda i: (i, 0)),
        out_shape=jax.ShapeDtypeStruct((D_MODEL, n_cols), BF16),
        compiler_params=_cparams(("parallel",)),
        name="w_in_pack",
    )(w_in)


def _to_block_diag(s):
    b = s.shape[0]
    s = s.reshape(b, N_PAIR_A, PAIR, HEAD_DIM, HEAD_DIM)
    eye = jnp.eye(PAIR, dtype=s.dtype)
    out = s[:, :, :, :, None, :] * eye[None, None, :, None, :, None]
    return out.reshape(b, N_PAIR_A, LANES, LANES)


def _from_block_diag(sbd):
    b = sbd.shape[0]
    s = sbd.reshape(b, N_PAIR_A, PAIR, HEAD_DIM, PAIR, HEAD_DIM)
    s = jnp.stack([s[:, :, j, :, j, :] for j in range(PAIR)], axis=2)
    return s.reshape(b, H_A, HEAD_DIM, HEAD_DIM)


def _layer_params(l, norm_w, w_in, w_out, rwkv_mu, rwkv_w0, rwkv_w2, rwkv_a0, rwkv_a2, rwkv_k_k,
                  rwkv_k_a, rwkv_r_k, rwkv_ln_w, rwkv_ln_b, ssm_conv_w, ssm_conv_b, ssm_dt_bias,
                  ssm_A_log, ssm_D, ssm_norm_w, sb_q_norm_w, sb_k_norm_w):
    row = lambda x: x.reshape(1, -1)
    pad_h = lambda x: jnp.pad(x, (0, DT_PAD - H_B)).reshape(1, DT_PAD)
    return {
        "norm_w": row(norm_w[l]), "w_in": _pack_w_in(w_in, l), "w_out": w_out[l].astype(BF16),
        "rwkv": {"mu": row(rwkv_mu[l]), "w0": row(rwkv_w0[l]), "w2": rwkv_w2[l].astype(BF16),
                 "a0": row(rwkv_a0[l]), "a2": rwkv_a2[l].astype(BF16), "k_k": row(rwkv_k_k[l]),
                 "k_a": row(rwkv_k_a[l]), "r_k": row(rwkv_r_k[l]), "ln_w": row(rwkv_ln_w[l]),
                 "ln_b": row(rwkv_ln_b[l])},
        "ssm": {"conv_w": ssm_conv_w[l], "conv_b": row(ssm_conv_b[l]),
                "dt_bias": pad_h(ssm_dt_bias[l]), "a_log": pad_h(ssm_A_log[l]),
                "d_x": row(jnp.repeat(ssm_D[l], HEAD_DIM)), "norm_w": row(ssm_norm_w[l])},
        "sb_qw": row(jnp.tile(sb_q_norm_w[l], PAIR)),
        "sb_kw": row(jnp.tile(sb_k_norm_w[l], H_C)),
    }


def _layer(x, prm, consts, B, T, t_valid, s_rwkv_bd, shift, s_ssm, conv_buf, k_past, v_past,
           layer, kv_all, tm_in, tm_kv, tm_out, tq):
    ua, ga, zb, xbc, dt, qc, kc, vc, gc = _in_proj(x, prm["norm_w"], prm["w_in"], tm_in)
    oa, s_rwkv_bd, shift = _rwkv(ua, ga, s_rwkv_bd, shift, prm["rwkv"], consts["tri"], B, T, t_valid)
    ob, s_ssm, conv_buf = _ssd(xbc, zb, dt, s_ssm, conv_buf, prm["ssm"], consts, B, T, t_valid)
    kn, vb, k_new, v_new = _kv_prep(kc, vc, prm["sb_kw"], B, T, tm_kv, layer, kv_all)
    if k_past is None:
        q_start, t_k = 0, T
    else:
        q_start = k_past.shape[1]
        t_k = -(-(q_start + T) // (2 * KB)) * (2 * KB)
        tail = jnp.zeros((B, t_k - q_start - T, D_C), BF16)
        cat = lambda past, new: jnp.concatenate(
            [past, new.reshape(B, T, D_C), tail], axis=1).reshape(B * t_k, D_C)
        kn, vb = cat(k_past, kn), cat(v_past, vb)
    oc = _sb_attn(qc, kn, vb, gc, prm["sb_qw"], consts["mcat"], B, T, t_k, tq, q_start)
    y = _out_proj(x, oa, ob, oc, prm["w_out"], tm_out)
    return y, (s_rwkv_bd, shift, s_ssm, conv_buf, k_new, v_new)


def _cache_pack_kernel(c_ref, o_ref):
    for p in range(N_PAIR_C):
        pair = jnp.concatenate([c_ref[0, p * PAIR + j] for j in range(PAIR)], axis=0)
        o_ref[0, :, p * LANES:(p + 1) * LANES] = pair.T.astype(o_ref.dtype)


def _packed_cache(c):
    b, h, p, d = c.shape
    tp = min(p, 512)
    assert p % tp == 0 and (h, d) == (H_C, HEAD_DIM)
    return pl.pallas_call(
        _cache_pack_kernel,
        grid=(b, p // tp),
        in_specs=[pl.BlockSpec((1, h, d, tp), lambda i, j: (i, 0, 0, j))],
        out_specs=pl.BlockSpec((1, tp, h * d), lambda i, j: (i, j, 0)),
        out_shape=jax.ShapeDtypeStruct((b, p, h * d), BF16),
        compiler_params=_cparams(("parallel", "parallel")),
        name="sb_cache_pack",
    )(jnp.swapaxes(c, -1, -2))


def kernel(x_prompt, x_sample, state_rwkv, state_rwkv_shift, state_ssm, state_conv, cache_sb_k, cache_sb_v,
           norm_w, w_in, w_out, rwkv_mu, rwkv_w0, rwkv_w2, rwkv_a0, rwkv_a2, rwkv_k_k, rwkv_k_a, rwkv_r_k,
           rwkv_ln_w, rwkv_ln_b, ssm_conv_w, ssm_conv_b, ssm_dt_bias, ssm_A_log, ssm_D, ssm_norm_w,
           sb_q_norm_w, sb_k_norm_w):
    bp, tp, _ = x_prompt.shape
    bs, ts, _ = x_sample.shape
    depth = w_in.shape[0]
    ts_pad = -(-ts // CHUNK) * CHUNK
    consts = _constants()

    yp = x_prompt.reshape(bp * tp, D_MODEL)
    ys = jnp.pad(x_sample, ((0, 0), (0, ts_pad - ts), (0, 0))).reshape(bs * ts_pad, D_MODEL)

    new_p = [[] for _ in range(4)]
    new_s = [[] for _ in range(4)]
    kv_p = (jnp.zeros((depth, bp, H_C, HEAD_DIM, tp), F32),) * 2
    kv_s = (jnp.zeros((depth, bs, H_C, HEAD_DIM, ts_pad), F32),) * 2
    for l in range(depth):
        prm = _layer_params(l, norm_w, w_in, w_out, rwkv_mu, rwkv_w0, rwkv_w2, rwkv_a0, rwkv_a2,
                            rwkv_k_k, rwkv_k_a, rwkv_r_k, rwkv_ln_w, rwkv_ln_b, ssm_conv_w, ssm_conv_b,
                            ssm_dt_bias, ssm_A_log, ssm_D, ssm_norm_w, sb_q_norm_w, sb_k_norm_w)
        yp, st_p = _layer(
            yp, prm, consts, bp, tp, tp,
            jnp.zeros((bp, N_PAIR_A, LANES, LANES), F32), jnp.zeros((bp, 1, W_SHIFT), F32),
            jnp.zeros((bp, N_PAIR_B, LANES, D_STATE), F32), jnp.zeros((bp, CONV_W - 1, CONV_DIM), F32),
            None, None, l, kv_p, tm_in=256, tm_kv=256, tm_out=512, tq=KG * KB)
        ys, st_s = _layer(
            ys, prm, consts, bs, ts_pad, ts,
            _to_block_diag(state_rwkv[l]), state_rwkv_shift[l],
            state_ssm[l].reshape(bs, N_PAIR_B, LANES, D_STATE), state_conv[l],
            _packed_cache(cache_sb_k[l]), _packed_cache(cache_sb_v[l]),
            l, kv_s, tm_in=256, tm_kv=ts_pad, tm_out=512, tq=ts_pad)
        kv_p, kv_s = st_p[4:], st_s[4:]
        for i in range(4):
            new_p[i].append(st_p[i])
            new_s[i].append(st_s[i])

    def finish(y, st, kv_all, b, t_pad, t):
        s_rwkv, shift, s_ssm, conv = [jnp.stack(v) for v in st]
        k_new, v_new = [jnp.swapaxes(a, -1, -2) for a in kv_all]
        return (y.reshape(b, t_pad, D_MODEL)[:, :t],
                (_from_block_diag(s_rwkv.reshape(depth * b, N_PAIR_A, LANES, LANES))
                 .reshape(depth, b, H_A, HEAD_DIM, HEAD_DIM)),
                shift,
                s_ssm.reshape(depth, b, H_B, HEAD_DIM, D_STATE),
                conv,
                k_new[:, :, :, :t],
                v_new[:, :, :, :t])

    yp, *rest_p = finish(yp, new_p, kv_p, bp, tp, tp)
    ys, *rest_s = finish(ys, new_s, kv_s, bs, ts_pad, ts)
    return (yp, ys, *rest_p, *rest_s)
```

```python
import functools

import jax
import jax.numpy as jnp
from jax import lax
from jax.experimental import pallas as pl
from jax.experimental.pallas import tpu as pltpu

F32 = jnp.float32
BF16 = jnp.bfloat16

D_MODEL = 1024
D_MIX = 2 * D_MODEL
HEAD_DIM = 64
D_A = 768
H_A = D_A // HEAD_DIM
R_W = 64
R_A = 64
GN_EPS = 64e-5
D_B = 768
H_B = D_B // HEAD_DIM
N_GROUPS = 2
D_STATE = 128
CONV_W = 4
CONV_DIM = D_B + 2 * N_GROUPS * D_STATE
D_C = 512
H_C = D_C // HEAD_DIM
W_SHIFT = 3 * D_A + R_W + R_A
SB_SCALE = HEAD_DIM ** -0.5

LANES = 128
PAIR = LANES // HEAD_DIM
CHUNK = 64
VMEM_LIMIT = 56 * 1024 * 1024

N_PAIR_A = H_A // PAIR
N_PAIR_B = H_B // PAIR
N_PAIR_C = H_C // PAIR
DT_PAD = LANES

IN_WIDTHS = (W_SHIFT, D_A, D_B, CONV_DIM, DT_PAD, D_C, D_C, D_C, D_C)


def _cparams(sem):
    return pltpu.CompilerParams(dimension_semantics=sem, vmem_limit_bytes=VMEM_LIMIT)


def _bdot(a, b):
    return jnp.dot(a.astype(BF16), b.astype(BF16), preferred_element_type=F32)


def _bdot_nt(a, b):
    return lax.dot_general(a.astype(BF16), b.astype(BF16), (((1,), (1,)), ((), ())),
                           preferred_element_type=F32)


def _bdot_tn(a, b):
    return lax.dot_general(a.astype(BF16), b.astype(BF16), (((0,), (0,)), ((), ())),
                           preferred_element_type=F32)


def _split_bf16(x, n):
    parts, r = [], x
    for i in range(n):
        p = r.astype(BF16)
        parts.append(p)
        if i + 1 < n:
            r = r - p.astype(F32)
    return parts


def _sel_dot(sel, x, n=3):
    acc = None
    for p in _split_bf16(x, n):
        d = jnp.dot(sel, p, preferred_element_type=F32)
        acc = d if acc is None else acc + d
    return acc


def _dot_sel(x, sel, n=3):
    acc = None
    for p in _split_bf16(x, n):
        d = jnp.dot(p, sel, preferred_element_type=F32)
        acc = d if acc is None else acc + d
    return acc


def _lane_lo(shape):
    return lax.broadcasted_iota(jnp.int32, shape, len(shape) - 1) < HEAD_DIM


def _head_sum(x, lo):
    s_lo = jnp.sum(jnp.where(lo, x, 0.0), axis=-1, keepdims=True)
    s_hi = jnp.sum(jnp.where(lo, 0.0, x), axis=-1, keepdims=True)
    return jnp.where(lo, s_lo, s_hi)


def _stack_heads(x, lo):
    zero = jnp.zeros_like(x)
    return jnp.concatenate([jnp.where(lo, x, zero), jnp.where(lo, zero, x)], axis=0)


def _silu(x):
    return x * jax.nn.sigmoid(x)


def _softplus(x):
    return jnp.maximum(x, 0.0) + jnp.log1p(jnp.exp(-jnp.abs(x)))


def _in_proj_kernel(x_ref, nw_ref, w_ref, *out_refs):
    x = x_ref[...]
    ms = jnp.mean(x * x, axis=-1, keepdims=True)
    h = (x * lax.rsqrt(ms + 1e-6) * nw_ref[...]).astype(BF16)
    off = 0
    for o_ref, wd in zip(out_refs, IN_WIDTHS):
        o_ref[...] = jnp.dot(h, w_ref[:, off:off + wd], preferred_element_type=F32)
        off += wd


def _in_proj(x, norm_w, w_cat, tm):
    n = x.shape[0]
    n_cols = sum(IN_WIDTHS)
    return pl.pallas_call(
        _in_proj_kernel,
        grid=(n // tm,),
        in_specs=[pl.BlockSpec((tm, D_MODEL), lambda i: (i, 0)),
                  pl.BlockSpec((1, D_MODEL), lambda i: (0, 0)),
                  pl.BlockSpec((D_MODEL, n_cols), lambda i: (0, 0))],
        out_specs=[pl.BlockSpec((tm, wd), lambda i: (i, 0)) for wd in IN_WIDTHS],
        out_shape=[jax.ShapeDtypeStruct((n, wd), F32) for wd in IN_WIDTHS],
        compiler_params=_cparams(("parallel",)),
        name="in_proj",
    )(x, norm_w, w_cat)


def _rwkv_kernel(ua_ref, ga_ref, s0_ref, sh0_ref, mu_ref, w0_ref, w2_ref, a0_ref, a2_ref,
                 kkw_ref, kaw_ref, rkw_ref, lnw_ref, lnb_ref, tri_ref,
                 oa_ref, st_ref, sht_ref,
                 s_scr, prev_scr, us_scr, *, n_t, last_valid):
    C = CHUNK
    t = pl.program_id(1)

    @pl.when(t == 0)
    def _():
        s_scr[...] = s0_ref[0]
        prev_scr[...] = sh0_ref[0]

    u = ua_ref[...]
    row = lax.broadcasted_iota(jnp.int32, (C, 1), 0)
    u_prev = jnp.where(row == 0, prev_scr[...], pltpu.roll(u, 1, axis=0))
    us_scr[...] = u + (u_prev - u) * mu_ref[...]
    prev_scr[...] = u[C - 1:C, :]

    padded = last_valid < C
    valid = jnp.logical_or(t < n_t - 1, row < last_valid) if padded else None

    w_lo = us_scr[:, 3 * D_A:3 * D_A + R_W]
    a_lo = us_scr[:, 3 * D_A + R_W:W_SHIFT]
    wl = w0_ref[...] + _bdot(jnp.tanh(w_lo), w2_ref[...])
    lw = -jnp.exp(-_softplus(-wl) - 0.5)
    a_all = jax.nn.sigmoid(a0_ref[...] + _bdot(a_lo, a2_ref[...]))
    if padded:
        lw = jnp.where(valid, lw, 0.0)
    cum_all = _sel_dot(tri_ref[...], lw)

    lo = _lane_lo((C, LANES))
    ri = lax.broadcasted_iota(jnp.int32, (2 * C, 2 * C), 0) & (C - 1)
    ci = lax.broadcasted_iota(jnp.int32, (2 * C, 2 * C), 1) & (C - 1)
    strict = ci < ri
    incl = ci <= ri

    pairs = range(N_PAIR_A)
    lanes = [slice(p * LANES, (p + 1) * LANES) for p in pairs]
    lhs, rhs, ends, vs, w_all, extra = [], [], [], [], [], []
    for p in pairs:
        sl = lanes[p]
        r = us_scr[:, p * LANES:(p + 1) * LANES]
        k = us_scr[:, D_A + p * LANES:D_A + (p + 1) * LANES]
        v = us_scr[:, 2 * D_A + p * LANES:2 * D_A + (p + 1) * LANES]
        a = a_all[:, sl]
        cum = cum_all[:, sl]
        kk = k * kkw_ref[:, sl]
        kkn = kk / jnp.maximum(jnp.sqrt(_head_sum(kk * kk, lo)), 1e-12)
        k2 = k * (1.0 + (a - 1.0) * kaw_ref[:, sl])
        vv = v
        if padded:
            kkn = jnp.where(valid, kkn, 0.0)
            k2 = jnp.where(valid, k2, 0.0)
            vv = jnp.where(valid, v, 0.0)
        cum_last = cum[C - 1:C, :]
        w_inc = jnp.exp(cum)
        w_exc = jnp.exp(cum - lw[:, sl])
        w_inv = jnp.exp(-cum)
        w_end = jnp.exp(cum_last - cum)
        w_all.append(jnp.exp(cum_last))
        ad = kkn * a
        lhs.append(jnp.concatenate(
            [_stack_heads(-kkn * w_exc, lo), _stack_heads(r * w_inc, lo)], axis=0).astype(BF16))
        rhs.append(jnp.concatenate(
            [_stack_heads(ad * w_inv, lo), _stack_heads(k2 * w_inv, lo)], axis=0).astype(BF16))
        ends.append(jnp.concatenate(
            [_stack_heads(ad * w_end, lo), _stack_heads(k2 * w_end, lo)], axis=0).astype(BF16))
        vs.append(_stack_heads(vv, lo).astype(BF16))
        bonus = _head_sum(r * k2 * rkw_ref[:, sl], lo)
        extra.append((bonus * v, _silu(ga_ref[:, sl])))

    g = [_bdot_nt(lhs[p], rhs[p]) for p in pairs]
    ps = [_bdot_nt(lhs[p], s_scr[p]) for p in pairs]
    x = [ps[p][0:2 * C] + _bdot(jnp.where(strict, g[p][0:2 * C, 2 * C:4 * C], 0.0), vs[p])
         for p in pairs]
    apow = [jnp.where(strict, g[p][0:2 * C, 0:2 * C], 0.0).astype(BF16) for p in pairs]
    n_sq = C.bit_length() - 1
    for i in range(n_sq):
        x = [x[p] + _bdot(apow[p], x[p]) for p in pairs]
        if i + 1 < n_sq:
            apow = [_bdot(apow[p], apow[p]).astype(BF16) for p in pairs]
    for p in pairs:
        sl = lanes[p]
        b_ab = jnp.where(incl, g[p][2 * C:4 * C, 0:2 * C], 0.0).astype(BF16)
        b_ak = jnp.where(incl, g[p][2 * C:4 * C, 2 * C:4 * C], 0.0).astype(BF16)
        uv = jnp.concatenate([x[p].astype(BF16), vs[p]], axis=0)
        ys = ps[p][2 * C:4 * C] + _bdot(jnp.concatenate([b_ab, b_ak], axis=1), uv)
        y = ys[0:C] + ys[C:2 * C]
        s_scr[p] = s_scr[p] * w_all[p] + _bdot_tn(uv, ends[p])

        mean = _head_sum(y, lo) * (1.0 / HEAD_DIM)
        d = y - mean
        var = _head_sum(d * d, lo) * (1.0 / HEAD_DIM)
        yn = d * lax.rsqrt(var + GN_EPS) * lnw_ref[:, sl] + lnb_ref[:, sl]
        bv, gate = extra[p]
        oa_ref[:, sl] = ((yn + bv) * gate).astype(oa_ref.dtype)

    @pl.when(t == n_t - 1)
    def _():
        st_ref[0] = s_scr[...]
        sht_ref[0] = ua_ref[last_valid - 1:last_valid, :]


def _rwkv(ua, ga, s0_bd, sh0, prm, tri, B, T, t_valid):
    n_t = T // CHUNK
    last_valid = t_valid - (n_t - 1) * CHUNK
    assert 1 <= last_valid <= CHUNK
    row = lambda b, t: (b * n_t + t, 0)
    const = lambda b, t: (0, 0)
    vec = lambda w: pl.BlockSpec((1, w), const)
    kern = functools.partial(_rwkv_kernel, n_t=n_t, last_valid=last_valid)
    return pl.pallas_call(
        kern,
        grid=(B, n_t),
        in_specs=[pl.BlockSpec((CHUNK, W_SHIFT), row),
                  pl.BlockSpec((CHUNK, D_A), row),
                  pl.BlockSpec((1, N_PAIR_A, LANES, LANES), lambda b, t: (b, 0, 0, 0)),
                  pl.BlockSpec((1, 1, W_SHIFT), lambda b, t: (b, 0, 0)),
                  vec(W_SHIFT), vec(D_A), pl.BlockSpec((R_W, D_A), const),
                  vec(D_A), pl.BlockSpec((R_A, D_A), const),
                  vec(D_A), vec(D_A), vec(D_A), vec(D_A), vec(D_A),
                  pl.BlockSpec((CHUNK, CHUNK), const)],
        out_specs=[pl.BlockSpec((CHUNK, D_A), row),
                   pl.BlockSpec((1, N_PAIR_A, LANES, LANES), lambda b, t: (b, 0, 0, 0)),
                   pl.BlockSpec((1, 1, W_SHIFT), lambda b, t: (b, 0, 0))],
        out_shape=[jax.ShapeDtypeStruct((B * T, D_A), BF16),
                   jax.ShapeDtypeStruct((B, N_PAIR_A, LANES, LANES), F32),
                   jax.ShapeDtypeStruct((B, 1, W_SHIFT), F32)],
        scratch_shapes=[pltpu.VMEM((N_PAIR_A, LANES, LANES), F32),
                        pltpu.VMEM((1, W_SHIFT), F32),
                        pltpu.VMEM((CHUNK, W_SHIFT), F32)],
        compiler_params=_cparams(("parallel", "arbitrary")),
        name="rwkv7_chunk",
    )(ua, ga, s0_bd, sh0, prm["mu"], prm["w0"], prm["w2"], prm["a0"], prm["a2"],
      prm["k_k"], prm["k_a"], prm["r_k"], prm["ln_w"], prm["ln_b"], tri)


CONV_PAD = 8


def _ssd_kernel(xbc_ref, z_ref, dt_ref, s0_ref, conv0_ref, cw_ref, cb_ref, dtb_ref, alog_ref,
                dx_ref, nw_ref, tri_ref, e64_ref, mrow_ref, mtril_ref,
                ob_ref, st_ref, convt_ref,
                s_scr, ext_scr, *, n_t, last_valid):
    C = CHUNK
    t = pl.program_id(1)
    n_prev = CONV_W - 1

    @pl.when(t == 0)
    def _():
        for p in range(N_PAIR_B):
            s_scr[p] = s0_ref[0, p].T
        ext_scr[CONV_PAD - n_prev:CONV_PAD, :] = conv0_ref[0]

    ext_scr[CONV_PAD:CONV_PAD + C, :] = xbc_ref[...]
    conv = cb_ref[...]
    for i in range(CONV_W):
        conv = conv + ext_scr[CONV_PAD - n_prev + i:CONV_PAD - n_prev + i + C, :] * cw_ref[i:i + 1, :]

    @pl.when(t == n_t - 1)
    def _():
        convt_ref[0] = ext_scr[CONV_PAD + last_valid - n_prev:CONV_PAD + last_valid, :]

    ext_scr[CONV_PAD - n_prev:CONV_PAD, :] = ext_scr[CONV_PAD + C - n_prev:CONV_PAD + C, :]

    xa = _silu(conv)
    xs = xa[:, 0:D_B]
    bm = xa[:, D_B:D_B + N_GROUPS * D_STATE]
    cm = xa[:, D_B + N_GROUPS * D_STATE:CONV_DIM]

    dtv = _softplus(dt_ref[...] + dtb_ref[...])
    if last_valid < C:
        row = lax.broadcasted_iota(jnp.int32, (C, 1), 0)
        dtv = jnp.where(jnp.logical_or(t < n_t - 1, row < last_valid), dtv, 0.0)
    da = dtv * (-jnp.exp(alog_ref[...]))
    a_cs = _sel_dot(tri_ref[...], da)
    ex = _dot_sel(jnp.concatenate([dtv, da, a_cs], axis=0), e64_ref[...])
    dt_x, da_x, acs_x = ex[0:C], ex[C:2 * C], ex[2 * C:3 * C]
    acs_row = jnp.sum(da_x * mrow_ref[...], axis=0, keepdims=True)
    seg = jnp.where(mtril_ref[...] > 0.0, jnp.exp(acs_x - acs_row), 0.0)

    hpg = H_B // N_GROUPS
    cb_tiles = []
    for g in range(N_GROUPS):
        b_g = bm[:, g * D_STATE:(g + 1) * D_STATE]
        c_g = cm[:, g * D_STATE:(g + 1) * D_STATE]
        cb_tiles.append(_bdot_nt(c_g, jnp.concatenate([b_g] * hpg, axis=0)))
    scores = jnp.concatenate(cb_tiles, axis=1) * seg
    xdt = xs * dt_x
    acs_last = acs_x[C - 1:C, :]
    e_in = jnp.exp(acs_x)
    e_end = jnp.exp(acs_last - acs_x)
    e_all = jnp.exp(acs_last)

    lo = _lane_lo((C, LANES))
    pairs_per_group = N_PAIR_B // N_GROUPS
    pairs = range(N_PAIR_B)
    lanes = [slice(p * LANES, (p + 1) * LANES) for p in pairs]
    b_bf = [bm[:, g * D_STATE:(g + 1) * D_STATE].astype(BF16) for g in range(N_GROUPS)]
    c_bf = [cm[:, g * D_STATE:(g + 1) * D_STATE].astype(BF16) for g in range(N_GROUPS)]
    y_diag = [_bdot(scores[:, lanes[p]], _stack_heads(xdt[:, lanes[p]], lo)) for p in pairs]
    y_off = [_bdot(c_bf[p // pairs_per_group], s_scr[p]) for p in pairs]
    s_add = [_bdot_tn(b_bf[p // pairs_per_group], xdt[:, lanes[p]] * e_end[:, lanes[p]]) for p in pairs]
    ys = []
    for p in pairs:
        sl = lanes[p]
        s_scr[p] = s_scr[p] * e_all[:, sl] + s_add[p]
        y = y_diag[p] + y_off[p] * e_in[:, sl] + dx_ref[:, sl] * xs[:, sl]
        ys.append(y * _silu(z_ref[:, sl]))

    gw = D_B // N_GROUPS
    for g in range(N_GROUPS):
        yg = jnp.concatenate(ys[g * pairs_per_group:(g + 1) * pairs_per_group], axis=1)
        ms = jnp.mean(yg * yg, axis=-1, keepdims=True)
        ob_ref[:, g * gw:(g + 1) * gw] = (
            yg * lax.rsqrt(ms + 1e-5) * nw_ref[:, g * gw:(g + 1) * gw]).astype(ob_ref.dtype)

    @pl.when(t == n_t - 1)
    def _():
        for p in range(N_PAIR_B):
            st_ref[0, p] = s_scr[p].T


def _ssd(xbc, zb, dt, s0, conv0, prm, consts, B, T, t_valid):
    n_t = T // CHUNK
    last_valid = t_valid - (n_t - 1) * CHUNK
    assert CONV_W - 1 <= last_valid <= CHUNK
    row = lambda b, t: (b * n_t + t, 0)
    const = lambda b, t: (0, 0)
    vec = lambda w: pl.BlockSpec((1, w), const)
    kern = functools.partial(_ssd_kernel, n_t=n_t, last_valid=last_valid)
    return pl.pallas_call(
        kern,
        grid=(B, n_t),
        in_specs=[pl.BlockSpec((CHUNK, CONV_DIM), row),
                  pl.BlockSpec((CHUNK, D_B), row),
                  pl.BlockSpec((CHUNK, DT_PAD), row),
                  pl.BlockSpec((1, N_PAIR_B, LANES, D_STATE), lambda b, t: (b, 0, 0, 0)),
                  pl.BlockSpec((1, CONV_W - 1, CONV_DIM), lambda b, t: (b, 0, 0)),
                  pl.BlockSpec((CONV_W, CONV_DIM), const), vec(CONV_DIM),
                  vec(DT_PAD), vec(DT_PAD), vec(D_B), vec(D_B),
                  pl.BlockSpec((CHUNK, CHUNK), const),
                  pl.BlockSpec((DT_PAD, D_B), const),
                  pl.BlockSpec((CHUNK, D_B), const),
                  pl.BlockSpec((CHUNK, D_B), const)],
        out_specs=[pl.BlockSpec((CHUNK, D_B), row),
                   pl.BlockSpec((1, N_PAIR_B, LANES, D_STATE), lambda b, t: (b, 0, 0, 0)),
                   pl.BlockSpec((1, CONV_W - 1, CONV_DIM), lambda b, t: (b, 0, 0))],
        out_shape=[jax.ShapeDtypeStruct((B * T, D_B), BF16),
                   jax.ShapeDtypeStruct((B, N_PAIR_B, LANES, D_STATE), F32),
                   jax.ShapeDtypeStruct((B, CONV_W - 1, CONV_DIM), F32)],
        scratch_shapes=[pltpu.VMEM((N_PAIR_B, D_STATE, LANES), F32),
                        pltpu.VMEM((CONV_PAD + CHUNK, CONV_DIM), F32)],
        compiler_params=_cparams(("parallel", "arbitrary")),
        name="ssd_chunk",
    )(xbc, zb, dt, s0, conv0, prm["conv_w"], prm["conv_b"], prm["dt_bias"], prm["a_log"],
      prm["d_x"], prm["norm_w"], consts["tri"], consts["e64"], consts["mrow"], consts["mtril"])


def _kv_prep_kernel(k_ref, v_ref, w_ref, kall_ref, vall_ref, kn_ref, vb_ref, knew_ref, vnew_ref):
    del kall_ref, vall_ref
    tm = k_ref.shape[0]
    lo = _lane_lo((tm, LANES))
    for p in range(N_PAIR_C):
        sl = slice(p * LANES, (p + 1) * LANES)
        k = k_ref[:, sl]
        ms = _head_sum(k * k, lo) * (1.0 / HEAD_DIM)
        kn = k * lax.rsqrt(ms + 1e-6) * w_ref[:, sl]
        v = v_ref[:, sl]
        kn_ref[:, sl] = kn.astype(kn_ref.dtype)
        vb_ref[:, sl] = v.astype(vb_ref.dtype)
        knt, vt = kn.T, v.T
        for j in range(PAIR):
            h = p * PAIR + j
            knew_ref[0, 0, h] = knt[j * HEAD_DIM:(j + 1) * HEAD_DIM, :]
            vnew_ref[0, 0, h] = vt[j * HEAD_DIM:(j + 1) * HEAD_DIM, :]


def _kv_prep(k, v, knw, B, T, tm, layer, kv_all):
    n_t = T // tm
    row = lambda b, t: (b * n_t + t, 0)
    hm = pl.BlockSpec((1, 1, H_C, HEAD_DIM, tm), lambda b, t: (layer, b, 0, 0, t))
    hm_shape = jax.ShapeDtypeStruct(kv_all[0].shape, F32)
    return pl.pallas_call(
        _kv_prep_kernel,
        grid=(B, n_t),
        in_specs=[pl.BlockSpec((tm, D_C), row), pl.BlockSpec((tm, D_C), row),
                  pl.BlockSpec((1, D_C), lambda b, t: (0, 0)),
                  pl.BlockSpec(memory_space=pl.ANY), pl.BlockSpec(memory_space=pl.ANY)],
        out_specs=[pl.BlockSpec((tm, D_C), row), pl.BlockSpec((tm, D_C), row), hm, hm],
        out_shape=[jax.ShapeDtypeStruct((B * T, D_C), BF16),
                   jax.ShapeDtypeStruct((B * T, D_C), BF16), hm_shape, hm_shape],
        input_output_aliases={3: 2, 4: 3},
        compiler_params=_cparams(("parallel", "parallel")),
        name="sb_kv_prep",
    )(k, v, knw, *kv_all)


KB = LANES


KG = 4
MASKED_LOGIT = -1e30


def _sb_attn_kernel(q_ref, k_ref, v_ref, g_ref, qw_ref, mcat_ref, o_ref, *, tq, q_start, n_q):
    qi = pl.program_id(2)
    gk = KG * KB
    lo = _lane_lo((tq, LANES))
    q = q_ref[...]
    ms = _head_sum(q * q, lo) * (1.0 / HEAD_DIM)
    qn = q * lax.rsqrt(ms + 1e-6) * qw_ref[...] * SB_SCALE
    zero = jnp.zeros_like(qn)
    q_both = jnp.concatenate([jnp.where(lo, qn, zero), jnp.where(lo, zero, qn)], axis=0).astype(BF16)
    q0 = q_start + qi * tq
    g_top = q0 // gk
    bp = 2 * KB
    n_bp_full = KG // 2

    def group(gi, carry, masked, n_bp):
        c_heads, acc = list(carry[:PAIR]), carry[PAIR]
        nk = n_bp * bp
        ks = pl.multiple_of(gi * gk, gk)
        kg = k_ref[pl.ds(ks, nk), :]
        vg = v_ref[pl.ds(ks, nk), :]
        lo_kv = _lane_lo((nk, LANES))
        z = _bdot_nt(q_both, kg)
        if masked:
            k_pos = ks + lax.broadcasted_iota(jnp.int32, (PAIR * tq, nk), 1)
            q_pos = q0 + (lax.broadcasted_iota(jnp.int32, (PAIR * tq, nk), 0) & (tq - 1))
            z = jnp.where(k_pos < q_pos, z, MASKED_LOGIT)
        zb = z.astype(BF16)
        l = jnp.log(1.0 + jnp.exp(-jnp.abs(zb)))
        sp = jnp.maximum(zb, 0.0) + l
        d = jnp.minimum(zb, 0.0) - l
        tiles = [sp[h * tq:(h + 1) * tq, j * bp:(j + 1) * bp] for h in range(PAIR) for j in range(n_bp)]
        tc = jnp.dot(jnp.concatenate(tiles, axis=0), mcat_ref[...], preferred_element_type=F32)
        atts = []
        for h in range(PAIR):
            c = c_heads[h]
            att_h = [None] * n_bp
            for j in reversed(range(n_bp)):
                t0 = (h * n_bp + j) * tq
                rows = slice(h * tq, (h + 1) * tq)
                tail = tc[t0:t0 + tq, :]
                ex = (tail + c).astype(BF16) + d[rows, j * bp:(j + 1) * bp]
                att_h[j] = jnp.exp(ex)
                c = c + (tail[:, 0:1] - sp[rows, j * bp:j * bp + 1].astype(F32))
            c_heads[h] = c
            atts.extend(att_h)
        vz = jnp.zeros_like(vg)
        v_st = jnp.concatenate([jnp.where(lo_kv, vg, vz), jnp.where(lo_kv, vz, vg)], axis=0)
        acc = acc + jnp.dot(jnp.concatenate(atts, axis=1), v_st, preferred_element_type=F32)
        return (*c_heads, acc)

    init = (jnp.zeros((tq, 1), F32),) * PAIR + (jnp.zeros((tq, LANES), F32),)
    assert n_q == 1 or tq == gk
    carry = group(g_top, init, True, -(-(q_start % gk + tq) // bp))
    carry = lax.fori_loop(0, g_top, lambda i, c: group(g_top - 1 - i, c, False, n_bp_full), carry)
    o_ref[...] = (carry[PAIR] * _silu(g_ref[...])).astype(o_ref.dtype)


def _sb_attn(q, kn, vb, gc, qw, mcat, B, Tq, Tk, tq, q_start):
    nq = Tq // tq
    gk = KG * KB
    assert gk % tq == 0 and q_start % gk == 0 and tq & (tq - 1) == 0
    assert Tk % (2 * KB) == 0 and Tk >= q_start + Tq
    qrow = lambda b, p, i: (b * nq + i, p)
    kv = lambda b, p, i: (b, p)
    kern = functools.partial(_sb_attn_kernel, tq=tq, q_start=q_start, n_q=nq)
    return pl.pallas_call(
        kern,
        grid=(B, N_PAIR_C, nq),
        in_specs=[pl.BlockSpec((tq, LANES), qrow),
                  pl.BlockSpec((Tk, LANES), kv),
                  pl.BlockSpec((Tk, LANES), kv),
                  pl.BlockSpec((tq, LANES), qrow),
                  pl.BlockSpec((1, LANES), lambda b, p, i: (0, 0)),
                  pl.BlockSpec((2 * KB, 2 * KB), lambda b, p, i: (0, 0))],
        out_specs=pl.BlockSpec((tq, LANES), qrow),
        out_shape=jax.ShapeDtypeStruct((B * Tq, D_C), BF16),
        compiler_params=_cparams(("parallel", "parallel", "arbitrary")),
        name="sb_attn",
    )(q, kn, vb, gc, qw, mcat)


def _out_proj_kernel(x_ref, oa_ref, ob_ref, oc_ref, w_ref, y_ref):
    acc = jnp.dot(oa_ref[...], w_ref[0:D_A, :], preferred_element_type=F32)
    acc = acc + jnp.dot(ob_ref[...], w_ref[D_A:D_A + D_B, :], preferred_element_type=F32)
    acc = acc + jnp.dot(oc_ref[...], w_ref[D_A + D_B:D_MIX, :], preferred_element_type=F32)
    y_ref[...] = x_ref[...] + acc


def _out_proj(x, oa, ob, oc, w_out, tm):
    n = x.shape[0]
    blk = lambda w: pl.BlockSpec((tm, w), lambda i: (i, 0))
    return pl.pallas_call(
        _out_proj_kernel,
        grid=(n // tm,),
        in_specs=[blk(D_MODEL), blk(D_A), blk(D_B), blk(D_C),
                  pl.BlockSpec((D_MIX, D_MODEL), lambda i: (0, 0))],
        out_specs=blk(D_MODEL),
        out_shape=jax.ShapeDtypeStruct((n, D_MODEL), F32),
        compiler_params=_cparams(("parallel",)),
        name="out_proj",
    )(x, oa, ob, oc, w_out)


def _constants():
    i64 = jnp.arange(CHUNK)
    lane = jnp.arange(D_B)
    return {
        "tri": (i64[:, None] >= i64[None, :]).astype(BF16),
        "e64": (jnp.arange(DT_PAD)[:, None] == lane[None, :] // HEAD_DIM).astype(BF16),
        "mrow": (i64[:, None] <= lane[None, :] % CHUNK).astype(F32),
        "mtril": (lane[None, :] % CHUNK <= i64[:, None]).astype(F32),
        "mcat": -(jnp.arange(2 * KB)[:, None] > jnp.arange(2 * KB)[None, :]).astype(BF16),
    }


N_IN = W_SHIFT + D_A + D_B + CONV_DIM + H_B + 4 * D_C
DT_OFF = W_SHIFT + D_A + D_B + CONV_DIM


def _pack_w_in_kernel(w_ref, o_ref):
    o_ref[:, 0:DT_OFF] = w_ref[0, :, 0:DT_OFF].astype(o_ref.dtype)
    dt_tile = w_ref[0, :, DT_OFF:DT_OFF + DT_PAD]
    lane = lax.broadcasted_iota(jnp.int32, dt_tile.shape, 1)
    o_ref[:, DT_OFF:DT_OFF + DT_PAD] = jnp.where(lane < H_B, dt_tile, 0.0).astype(o_ref.dtype)
    for t in range(4 * D_C // LANES):
        src = DT_OFF + H_B + t * LANES
        dst = DT_OFF + DT_PAD + t * LANES
        o_ref[:, dst:dst + LANES] = w_ref[0, :, src:src + LANES].astype(o_ref.dtype)


def _pack_w_in(w_in, layer):
    assert w_in.shape[1:] == (D_MODEL, N_IN) and DT_OFF % LANES == 0
    tr = 128
    n_cols = sum(IN_WIDTHS)
    return pl.pallas_call(
        _pack_w_in_kernel,
        grid=(D_MODEL // tr,),
        in_specs=[pl.BlockSpec((1, tr, N_IN), lambda i: (layer, i, 0))],
        out_specs=pl.BlockSpec((tr, n_cols), lambda i: (i, 0)),
        out_shape=jax.ShapeDtypeStruct((D_MODEL, n_cols), BF16),
        compiler_params=_cparams(("parallel",)),
        name="w_in_pack",
    )(w_in)


def _to_block_diag(s):
    b = s.shape[0]
    s = s.reshape(b, N_PAIR_A, PAIR, HEAD_DIM, HEAD_DIM)
    eye = jnp.eye(PAIR, dtype=s.dtype)
    out = s[:, :, :, :, None, :] * eye[None, None, :, None, :, None]
    return out.reshape(b, N_PAIR_A, LANES, LANES)


def _from_block_diag(sbd):
    b = sbd.shape[0]
    s = sbd.reshape(b, N_PAIR_A, PAIR, HEAD_DIM, PAIR, HEAD_DIM)
    s = jnp.stack([s[:, :, j, :, j, :] for j in range(PAIR)], axis=2)
    return s.reshape(b, H_A, HEAD_DIM, HEAD_DIM)


def _layer_params(l, norm_w, w_in, w_out, rwkv_mu, rwkv_w0, rwkv_w2, rwkv_a0, rwkv_a2, rwkv_k_k,
                  rwkv_k_a, rwkv_r_k, rwkv_ln_w, rwkv_ln_b, ssm_conv_w, ssm_conv_b, ssm_dt_bias,
                  ssm_A_log, ssm_D, ssm_norm_w, sb_q_norm_w, sb_k_norm_w):
    row = lambda x: x.reshape(1, -1)
    pad_h = lambda x: jnp.pad(x, (0, DT_PAD - H_B)).reshape(1, DT_PAD)
    return {
        "norm_w": row(norm_w[l]), "w_in": _pack_w_in(w_in, l), "w_out": w_out[l].astype(BF16),
        "rwkv": {"mu": row(rwkv_mu[l]), "w0": row(rwkv_w0[l]), "w2": rwkv_w2[l].astype(BF16),
                 "a0": row(rwkv_a0[l]), "a2": rwkv_a2[l].astype(BF16), "k_k": row(rwkv_k_k[l]),
                 "k_a": row(rwkv_k_a[l]), "r_k": row(rwkv_r_k[l]), "ln_w": row(rwkv_ln_w[l]),
                 "ln_b": row(rwkv_ln_b[l])},
        "ssm": {"conv_w": ssm_conv_w[l], "conv_b": row(ssm_conv_b[l]),
                "dt_bias": pad_h(ssm_dt_bias[l]), "a_log": pad_h(ssm_A_log[l]),
                "d_x": row(jnp.repeat(ssm_D[l], HEAD_DIM)), "norm_w": row(ssm_norm_w[l])},
        "sb_qw": row(jnp.tile(sb_q_norm_w[l], PAIR)),
        "sb_kw": row(jnp.tile(sb_k_norm_w[l], H_C)),
    }


def _layer(x, prm, consts, B, T, t_valid, s_rwkv_bd, shift, s_ssm, conv_buf, k_past, v_past,
           layer, kv_all, tm_in, tm_kv, tm_out, tq):
    ua, ga, zb, xbc, dt, qc, kc, vc, gc = _in_proj(x, prm["norm_w"], prm["w_in"], tm_in)
    oa, s_rwkv_bd, shift = _rwkv(ua, ga, s_rwkv_bd, shift, prm["rwkv"], consts["tri"], B, T, t_valid)
    ob, s_ssm, conv_buf = _ssd(xbc, zb, dt, s_ssm, conv_buf, prm["ssm"], consts, B, T, t_valid)
    kn, vb, k_new, v_new = _kv_prep(kc, vc, prm["sb_kw"], B, T, tm_kv, layer, kv_all)
    if k_past is None:
        q_start, t_k = 0, T
    else:
        q_start = k_past.shape[1]
        t_k = -(-(q_start + T) // (2 * KB)) * (2 * KB)
        tail = jnp.zeros((B, t_k - q_start - T, D_C), BF16)
        cat = lambda past, new: jnp.concatenate(
            [past, new.reshape(B, T, D_C), tail], axis=1).reshape(B * t_k, D_C)
        kn, vb = cat(k_past, kn), cat(v_past, vb)
    oc = _sb_attn(qc, kn, vb, gc, prm["sb_qw"], consts["mcat"], B, T, t_k, tq, q_start)
    y = _out_proj(x, oa, ob, oc, prm["w_out"], tm_out)
    return y, (s_rwkv_bd, shift, s_ssm, conv_buf, k_new, v_new)


def _cache_pack_kernel(c_ref, o_ref):
    for p in range(N_PAIR_C):
        pair = jnp.concatenate([c_ref[0, 0, p * PAIR + j] for j in range(PAIR)], axis=0)
        o_ref[0, :, p * LANES:(p + 1) * LANES] = pair.T.astype(o_ref.dtype)


def _packed_cache(c, layer):
    _, b, h, p, d = c.shape
    tp = min(p, 512)
    assert p % tp == 0 and (h, d) == (H_C, HEAD_DIM)
    return pl.pallas_call(
        _cache_pack_kernel,
        grid=(b, p // tp),
        in_specs=[pl.BlockSpec((1, 1, h, d, tp), lambda i, j: (layer, i, 0, 0, j))],
        out_specs=pl.BlockSpec((1, tp, h * d), lambda i, j: (i, j, 0)),
        out_shape=jax.ShapeDtypeStruct((b, p, h * d), BF16),
        compiler_params=_cparams(("parallel", "parallel")),
        name="sb_cache_pack",
    )(jnp.swapaxes(c, -1, -2))


def kernel(x_prompt, x_sample, state_rwkv, state_rwkv_shift, state_ssm, state_conv, cache_sb_k, cache_sb_v,
           norm_w, w_in, w_out, rwkv_mu, rwkv_w0, rwkv_w2, rwkv_a0, rwkv_a2, rwkv_k_k, rwkv_k_a, rwkv_r_k,
           rwkv_ln_w, rwkv_ln_b, ssm_conv_w, ssm_conv_b, ssm_dt_bias, ssm_A_log, ssm_D, ssm_norm_w,
           sb_q_norm_w, sb_k_norm_w):
    bp, tp, _ = x_prompt.shape
    bs, ts, _ = x_sample.shape
    depth = w_in.shape[0]
    ts_pad = -(-ts // CHUNK) * CHUNK
    consts = _constants()

    yp = x_prompt.reshape(bp * tp, D_MODEL)
    ys = jnp.pad(x_sample, ((0, 0), (0, ts_pad - ts), (0, 0))).reshape(bs * ts_pad, D_MODEL)

    new_p = [[] for _ in range(4)]
    new_s = [[] for _ in range(4)]
    kv_p = (jnp.zeros((depth, bp, H_C, HEAD_DIM, tp), F32),) * 2
    kv_s = (jnp.zeros((depth, bs, H_C, HEAD_DIM, ts_pad), F32),) * 2
    for l in range(depth):
        prm = _layer_params(l, norm_w, w_in, w_out, rwkv_mu, rwkv_w0, rwkv_w2, rwkv_a0, rwkv_a2,
                            rwkv_k_k, rwkv_k_a, rwkv_r_k, rwkv_ln_w, rwkv_ln_b, ssm_conv_w, ssm_conv_b,
                            ssm_dt_bias, ssm_A_log, ssm_D, ssm_norm_w, sb_q_norm_w, sb_k_norm_w)
        yp, st_p = _layer(
            yp, prm, consts, bp, tp, tp,
            jnp.zeros((bp, N_PAIR_A, LANES, LANES), F32), jnp.zeros((bp, 1, W_SHIFT), F32),
            jnp.zeros((bp, N_PAIR_B, LANES, D_STATE), F32), jnp.zeros((bp, CONV_W - 1, CONV_DIM), F32),
            None, None, l, kv_p, tm_in=256, tm_kv=256, tm_out=512, tq=KG * KB)
        ys, st_s = _layer(
            ys, prm, consts, bs, ts_pad, ts,
            _to_block_diag(state_rwkv[l]), state_rwkv_shift[l],
            state_ssm[l].reshape(bs, N_PAIR_B, LANES, D_STATE), state_conv[l],
            _packed_cache(cache_sb_k, l), _packed_cache(cache_sb_v, l),
            l, kv_s, tm_in=256, tm_kv=ts_pad, tm_out=512, tq=ts_pad)
        kv_p, kv_s = st_p[4:], st_s[4:]
        for i in range(4):
            new_p[i].append(st_p[i])
            new_s[i].append(st_s[i])

    def finish(y, st, kv_all, b, t_pad, t):
        s_rwkv, shift, s_ssm, conv = [jnp.stack(v) for v in st]
        k_new, v_new = [jnp.swapaxes(a, -1, -2) for a in kv_all]
        return (y.reshape(b, t_pad, D_MODEL)[:, :t],
                (_from_block_diag(s_rwkv.reshape(depth * b, N_PAIR_A, LANES, LANES))
                 .reshape(depth, b, H_A, HEAD_DIM, HEAD_DIM)),
                shift,
                s_ssm.reshape(depth, b, H_B, HEAD_DIM, D_STATE),
                conv,
                k_new[:, :, :, :t],
                v_new[:, :, :, :t])

    yp, *rest_p = finish(yp, new_p, kv_p, bp, tp, tp)
    ys, *rest_s = finish(ys, new_s, kv_s, bs, ts_pad, ts)
    return (yp, ys, *rest_p, *rest_s)
```
